```python
import math
import jax, jax.numpy as jnp
from jax import lax
import numpy as np

D_MODEL = 2048
BATCH = 16
SEQ = 256
DEPTH = 2
DEC_BATCH = 8
DEC_SEQ = 1024
PAST_LEN = 512

GRID_W = 64
NA_HEADS = 16
HEAD_DIM = 64
NA_WIDTH = NA_HEADS * HEAD_DIM
POOL_WINDOWS = (2, 4, 8, 16)
N_POOL = len(POOL_WINDOWS)
POOL_WIDTH = D_MODEL // 2
POOL_GROUP = POOL_WIDTH // N_POOL
MIX_WIDTH = NA_WIDTH + POOL_WIDTH
IN_WIDTH = 3 * NA_WIDTH + POOL_WIDTH
NA_MAX_ROWS = 8
NA_COLS = 16
Q_BLOCK_ROWS_MAX = 8
Q_BLOCK_COLS = 16
CTX_Q_BLOCK = 128
N_EXPERTS = 16
EC_CAPACITY_FACTOR = 2
D_EXPERT = 1024
N_MOD = 6
RMS_EPS = 1e-6
NEG_INF = -1e30

kernel_name = "hybrid_na_pool_ec_diffusion_step"


def rms_norm(x, gain):
    xf = x.astype(jnp.float32)
    y = xf * lax.rsqrt(jnp.mean(xf * xf, axis=-1, keepdims=True) + RMS_EPS)
    return (y * gain.astype(jnp.float32)).astype(x.dtype)


def adaln(cond, w_mod, b_mod):
    m = jax.nn.silu(cond) @ w_mod + b_mod
    return jnp.split(m, N_MOD, axis=-1)


def modulate(h, shift, scale):
    return h * (1 + scale[:, None]) + shift[:, None]


def mixer_proj(h, w_in):
    B, T = h.shape[:2]
    p = h @ w_in
    q, k, v, u = jnp.split(p, [NA_WIDTH, 2 * NA_WIDTH, 3 * NA_WIDTH], axis=-1)
    hd = lambda a: a.reshape(B, T, NA_HEADS, HEAD_DIM)
    return hd(q), hd(k), hd(v), u


def centred_mean(u, w):
    T = u.shape[1]
    uf = u.astype(jnp.float32)
    cs = jnp.concatenate([jnp.zeros_like(uf[:, :1]), jnp.cumsum(uf, axis=1)], axis=1)
    t = jnp.arange(T)
    lo = jnp.maximum(t - w // 2, 0)
    hi = jnp.minimum(t + (w - w // 2), T)
    s = jnp.take(cs, hi, axis=1) - jnp.take(cs, lo, axis=1)
    cnt = (hi - lo).astype(jnp.float32)
    return (s / cnt[None, :, None]).astype(u.dtype)


def pool_mixer(u, w_pool, pool_scale):
    groups = jnp.split(u, N_POOL, axis=-1)
    outs = [(centred_mean(g, w) - g) @ w_pool[i] for i, (g, w) in enumerate(zip(groups, POOL_WINDOWS))]
    return jnp.concatenate(outs, axis=-1) * pool_scale


def context_attention(q, k, v):
    B, S, H, dh = q.shape
    nb = S // CTX_Q_BLOCK
    qb = q.reshape(B, nb, CTX_Q_BLOCK, H, dh).transpose(1, 0, 2, 3, 4)
    scale = HEAD_DIM ** -0.5

    def block(qblk):
        s = jnp.einsum('bqhd,bkhd->bhqk', qblk, k, preferred_element_type=jnp.float32) * scale
        p = jax.nn.softmax(s, axis=-1).astype(v.dtype)
        return jnp.einsum('bhqk,bkhd->bqhd', p, v)

    o = lax.map(block, qb)
    return o.transpose(1, 0, 2, 3, 4).reshape(B, S, H * dh)


def neighbourhood_attention(q, k, v, k_ctx, v_ctx, rpb):
    B, T, H, dh = q.shape
    rows = T // GRID_W
    kh = min(NA_MAX_ROWS, rows)
    kw = NA_COLS
    qbh = math.gcd(rows, Q_BLOCK_ROWS_MAX)
    qbw = Q_BLOCK_COLS
    kr = min(rows, kh + qbh - 1)
    kbw = min(GRID_W, kw + qbw - 1)
    nrb = rows // qbh
    ncb = GRID_W // qbw
    row_start = jnp.clip(jnp.arange(rows) - kh // 2, 0, rows - kh)
    col_start = jnp.clip(jnp.arange(GRID_W) - kw // 2, 0, GRID_W - kw)
    blk_row0 = jnp.clip(jnp.arange(nrb) * qbh - kh // 2, 0, rows - kr)
    blk_col0 = jnp.clip(jnp.arange(ncb) * qbw - kw // 2, 0, GRID_W - kbw)
    key_rows = blk_row0[:, None] + jnp.arange(kr)
    key_cols = blk_col0[:, None] + jnp.arange(kbw)
    q_rows = jnp.arange(nrb)[:, None] * qbh + jnp.arange(qbh)
    q_cols = jnp.arange(ncb)[:, None] * qbw + jnp.arange(qbw)
    rs = row_start[q_rows][:, :, None]
    cs = col_start[q_cols][:, :, None]
    in_row = (key_rows[:, None, :] >= rs) & (key_rows[:, None, :] < rs + kh)
    in_col = (key_cols[:, None, :] >= cs) & (key_cols[:, None, :] < cs + kw)
    d_row = jnp.clip(key_rows[:, None, :] - q_rows[:, :, None] + (NA_MAX_ROWS - 1), 0, 2 * NA_MAX_ROWS - 2)
    d_col = jnp.clip(key_cols[:, None, :] - q_cols[:, :, None] + (NA_COLS - 1), 0, 2 * NA_COLS - 2)
    mask = in_row[:, None, :, None, :, None] & in_col[None, :, None, :, None, :]
    bias = rpb[:, d_row[:, None, :, None, :, None], d_col[None, :, None, :, None, :]].astype(jnp.float32)
    bias = jnp.where(mask[None], bias, NEG_INF).reshape(H, nrb, ncb, qbh * qbw, kr * kbw)

    qb = q.reshape(B, nrb, qbh, ncb, qbw, H, dh).transpose(0, 1, 3, 2, 4, 5, 6).reshape(B, nrb, ncb, qbh * qbw, H, dh)
    ri = key_rows[:, None, :, None]
    ci = key_cols[None, :, None, :]
    kg = k.reshape(B, rows, GRID_W, H, dh)[:, ri, ci].reshape(B, nrb, ncb, kr * kbw, H, dh)
    vg = v.reshape(B, rows, GRID_W, H, dh)[:, ri, ci].reshape(B, nrb, ncb, kr * kbw, H, dh)
    scale = HEAD_DIM ** -0.5
    s_lat = jnp.einsum('bijqhd,bijkhd->bhijqk', qb, kg, preferred_element_type=jnp.float32) * scale + bias[None]
    s_ctx = jnp.einsum('bijqhd,bkhd->bhijqk', qb, k_ctx, preferred_element_type=jnp.float32) * scale
    n_lat = kr * kbw
    p = jax.nn.softmax(jnp.concatenate([s_lat, s_ctx], axis=-1), axis=-1).astype(v.dtype)
    o = (jnp.einsum('bhijqk,bijkhd->bijqhd', p[..., :n_lat], vg)
         + jnp.einsum('bhijqk,bkhd->bijqhd', p[..., n_lat:], v_ctx))
    o = o.reshape(B, nrb, ncb, qbh, qbw, H, dh).transpose(0, 1, 3, 2, 4, 5, 6)
    return o.reshape(B, T, H * dh)


def expert_choice_ffn(x, w_router, w_gate, w_up, w_down):
    B, n, D = x.shape
    cap = EC_CAPACITY_FACTOR * n // N_EXPERTS
    logits = jnp.einsum('bnd,de->ben', x, w_router, preferred_element_type=jnp.float32)
    aff = jax.nn.softmax(logits, axis=1)
    g, idx = lax.top_k(aff, cap)
    xs = jax.vmap(lambda xb, ib: xb[ib])(x, idx)
    hcur = jax.nn.silu(jnp.einsum('becd,edf->becf', xs, w_gate)) * jnp.einsum('becd,edf->becf', xs, w_up)
    ys = jnp.einsum('becf,efd->becd', hcur, w_down) * g[..., None].astype(x.dtype)
    return jax.vmap(lambda yb, ib: jnp.zeros((n, D), yb.dtype).at[ib.reshape(-1)].add(yb.reshape(-1, D)))(ys, idx)


def setup_inputs(seed: int = 0) -> dict:
    key = jax.random.key(seed)
    ks = jax.random.split(key, 20)
    nrm = lambda k, shape, s: jax.random.normal(k, shape, jnp.float32) * s
    return {
        'x_prompt': nrm(ks[0], (BATCH, SEQ, D_MODEL), 1.0),
        'x_sample': nrm(ks[1], (DEC_BATCH, DEC_SEQ, D_MODEL), 1.0),
        'cache_k': nrm(ks[2], (DEC_BATCH, DEPTH, PAST_LEN, NA_HEADS, HEAD_DIM), 1.0),
        'cache_v': nrm(ks[3], (DEC_BATCH, DEPTH, PAST_LEN, NA_HEADS, HEAD_DIM), 1.0),
        'c': nrm(ks[4], (DEC_BATCH, D_MODEL), 1.0),
        'c_ctx': nrm(ks[5], (D_MODEL,), 1.0),
        'w_mod': nrm(ks[6], (DEPTH, D_MODEL, N_MOD * D_MODEL), D_MODEL ** -0.5),
        'b_mod': nrm(ks[7], (DEPTH, N_MOD * D_MODEL), 0.02),
        'norm_mix': 1.0 + nrm(ks[8], (DEPTH, D_MODEL), 0.02),
        'norm_ffn': 1.0 + nrm(ks[9], (DEPTH, D_MODEL), 0.02),
        'w_in': nrm(ks[10], (DEPTH, D_MODEL, IN_WIDTH), D_MODEL ** -0.5),
        'rpb': nrm(ks[11], (DEPTH, NA_HEADS, 2 * NA_MAX_ROWS - 1, 2 * NA_COLS - 1), 0.1),
        'w_pool': nrm(ks[12], (DEPTH, N_POOL, POOL_GROUP, POOL_GROUP), POOL_GROUP ** -0.5),
        'pool_scale': 1.0 + nrm(ks[13], (DEPTH, POOL_WIDTH), 0.1),
        'w_out': nrm(ks[14], (DEPTH, MIX_WIDTH, D_MODEL), MIX_WIDTH ** -0.5),
        'w_router': nrm(ks[15], (DEPTH, D_MODEL, N_EXPERTS), D_MODEL ** -0.5),
        'w_gate': nrm(ks[16], (DEPTH, N_EXPERTS, D_MODEL, D_EXPERT), D_MODEL ** -0.5),
        'w_up': nrm(ks[17], (DEPTH, N_EXPERTS, D_MODEL, D_EXPERT), D_MODEL ** -0.5),
        'w_down': nrm(ks[18], (DEPTH, N_EXPERTS, D_EXPERT, D_MODEL), D_EXPERT ** -0.5),
        'norm_final': 1.0 + nrm(ks[19], (D_MODEL,), 0.02),
    }


def reference(x_prompt, x_sample, cache_k, cache_v, c, c_ctx, w_mod, b_mod, norm_mix, norm_ffn, w_in, rpb,
              w_pool, pool_scale, w_out, w_router, w_gate, w_up, w_down, norm_final):
    yp = x_prompt
    ys = x_sample
    new_k_layers = []
    new_v_layers = []
    for l in range(DEPTH):
        m_ctx = adaln(c_ctx[None], w_mod[l], b_mod[l])
        m_lat = adaln(c, w_mod[l], b_mod[l])

        h = modulate(rms_norm(yp, norm_mix[l]), m_ctx[0], m_ctx[1])
        q, k, v, u = mixer_proj(h, w_in[l])
        new_k_layers.append(k)
        new_v_layers.append(v)
        mix = jnp.concatenate([context_attention(q, k, v), pool_mixer(u, w_pool[l], pool_scale[l])], axis=-1) @ w_out[l]
        yp = yp + m_ctx[2][:, None] * mix
        h = modulate(rms_norm(yp, norm_ffn[l]), m_ctx[3], m_ctx[4])
        yp = yp + m_ctx[5][:, None] * expert_choice_ffn(h, w_router[l], w_gate[l], w_up[l], w_down[l])

        h = modulate(rms_norm(ys, norm_mix[l]), m_lat[0], m_lat[1])
        q, k, v, u = mixer_proj(h, w_in[l])
        att = neighbourhood_attention(q, k, v, cache_k[:, l], cache_v[:, l], rpb[l])
        mix = jnp.concatenate([att, pool_mixer(u, w_pool[l], pool_scale[l])], axis=-1) @ w_out[l]
        ys = ys + m_lat[2][:, None] * mix
        h = modulate(rms_norm(ys, norm_ffn[l]), m_lat[3], m_lat[4])
        ys = ys + m_lat[5][:, None] * expert_choice_ffn(h, w_router[l], w_gate[l], w_up[l], w_down[l])

    y_prompt = rms_norm(yp, norm_final)
    y_sample = rms_norm(ys, norm_final)
    new_k = jnp.stack(new_k_layers, axis=1)
    new_v = jnp.stack(new_v_layers, axis=1)
    return (y_prompt, y_sample, new_k, new_v)
```

```python
import functools

import jax
import jax.numpy as jnp
from jax import lax
from jax.experimental import pallas as pl
from jax.experimental.pallas import tpu as pltpu

F32 = jnp.float32
BF16 = jnp.bfloat16

D = 2048
BATCH, SEQ = 16, 256
DEPTH = 2
DEC_BATCH, DEC_SEQ = 8, 1024
PAST = 512
GRID_W = 64
H, DH = 16, 64
NA_W = H * DH
POOL_WINDOWS = (2, 4, 8, 16)
PG = 256
POOL_W = 1024
IN_W = 3 * NA_W + POOL_W
NA_ROWS, NA_COLS = 8, 16
E = 16
D_EXP = 1024
N_MOD = 6
RMS_EPS = 1e-6
NEG_INF = -1e30

N_CTX = BATCH * SEQ
N_LAT = DEC_BATCH * DEC_SEQ
N_TOK = N_CTX + N_LAT
UNIT = 1024
N_UNITS = N_TOK // UNIT
CTX_UNITS = N_CTX // UNIT
SLOTS = 128
CAP_CTX = 2 * SEQ // E
CAP_LAT = 2 * DEC_SEQ // E
N_SLOT = N_UNITS * SLOTS
N_ROW = 16
LANES = 128
MIB = 1024 * 1024


def _cparams(sem, vmem_mib):
    return pltpu.CompilerParams(dimension_semantics=sem, vmem_limit_bytes=vmem_mib * MIB)


def _resident(block_shape, index_map):
    return pl.BlockSpec(block_shape, index_map, pipeline_mode=pl.Buffered(1))


def _dot(a, b):
    return jnp.dot(a, b, preferred_element_type=F32)


def _dot_nt(a, b):
    return lax.dot_general(a, b, (((1,), (1,)), ((), ())), preferred_element_type=F32)


def _dot_tn(a, b):
    return lax.dot_general(a, b, (((0,), (0,)), ((), ())), preferred_element_type=F32)


def _silu(x):
    return x / (1.0 + jnp.exp(-x))


def _norm_mod(x, gain, shift, scale):
    y = x * lax.rsqrt(jnp.mean(x * x, axis=-1, keepdims=True) + RMS_EPS)
    return (y * gain) * (1.0 + scale) + shift


def _mod_row_of_tile(i, tile):
    per_req = DEC_SEQ // tile
    n_ctx_tiles = N_CTX // tile
    return jnp.where(i < n_ctx_tiles, 0, 1 + (i - n_ctx_tiles) // per_req)


def _mod_spec(layer, which, tile, grid_rank=1, axis=0):
    def imap(*ids):
        return ((layer * N_ROW + _mod_row_of_tile(ids[axis], tile)) * N_MOD + which, 0, 0)
    return pl.BlockSpec((1, 1, D), imap)


def _adaln_kernel(c_ref, w_ref, b_ref, o_ref):
    a = _silu(c_ref[...]).astype(BF16)
    o_ref[0] = _dot(a, w_ref[0].astype(BF16)) + b_ref[0]


def _adaln(cond, w_mod, b_mod):
    tn = 1024
    return pl.pallas_call(
        _adaln_kernel,
        grid=(DEPTH, N_MOD * D // tn),
        in_specs=[pl.BlockSpec((N_ROW, D), lambda l, j: (0, 0)),
                  pl.BlockSpec((1, D, tn), lambda l, j: (l, 0, j)),
                  pl.BlockSpec((1, 1, tn), lambda l, j: (l, 0, j))],
        out_specs=pl.BlockSpec((1, N_ROW, tn), lambda l, j: (l, 0, j)),
        out_shape=jax.ShapeDtypeStruct((DEPTH, N_ROW, N_MOD * D), F32),
        compiler_params=_cparams(("arbitrary", "arbitrary"), 40),
        name="adaln",
    )(cond, w_mod, b_mod.reshape(DEPTH, 1, N_MOD * D))


def _inproj_kernel(x_ref, g_ref, sh_ref, sc_ref, w_ref, o_ref, wb_ref, *, out_scale):
    @pl.when(pl.program_id(0) == 0)
    def _():
        wb_ref[...] = w_ref[...].astype(BF16)

    h = _norm_mod(x_ref[...], g_ref[0], sh_ref[0], sc_ref[0])
    acc = _dot(h.astype(BF16), wb_ref[...])
    if out_scale != 1.0:
        acc = acc * out_scale
    o_ref[...] = acc.astype(o_ref.dtype)


def _inproj(x, mods, norm_gain, w_in, layer, part, out_dtype, out_scale=1.0):
    tm, tn = 512, 1024
    return pl.pallas_call(
        functools.partial(_inproj_kernel, out_scale=out_scale),
        grid=(N_TOK // tm,),
        in_specs=[pl.BlockSpec((tm, D), lambda i: (i, 0)),
                  pl.BlockSpec((1, 1, D), lambda i: (layer, 0, 0)),
                  _mod_spec(layer, 0, tm),
                  _mod_spec(layer, 1, tm),
                  _resident((None, D, tn), lambda i: (layer, 0, part))],
        out_specs=pl.BlockSpec((tm, tn), lambda i: (i, 0)),
        out_shape=jax.ShapeDtypeStruct((N_TOK, tn), out_dtype),
        scratch_shapes=[pltpu.VMEM((D, tn), BF16)],
        compiler_params=_cparams(("arbitrary",), 44),
        name=f"inproj{part}",
    )(x, norm_gain, mods, mods, w_in)


def _softmax_pv(scores, values):
    m = functools.reduce(jnp.maximum, [jnp.max(s, axis=-1, keepdims=True) for s in scores])
    ps = [jnp.exp(s - m) for s in scores]
    l = functools.reduce(jnp.add, [jnp.sum(p, axis=-1, keepdims=True) for p in ps])
    o = functools.reduce(jnp.add, [_dot(p.astype(BF16), v) for p, v in zip(ps, values)])
    return o / l


def _ctx_attn_kernel(q_ref, k_ref, v_ref, o_ref):
    outs = []
    for hh in range(2):
        sl = slice(hh * DH, (hh + 1) * DH)
        k = k_ref[:, sl].astype(BF16)
        v = v_ref[:, sl].astype(BF16)
        outs.append(_softmax_pv([_dot_nt(q_ref[:, sl], k)], [v]))
    o_ref[...] = jnp.concatenate(outs, axis=-1).astype(o_ref.dtype)


def _ctx_attention(q, k, v):
    blk = lambda: pl.BlockSpec((SEQ, 2 * DH), lambda b, g: (b, g))
    return pl.pallas_call(
        _ctx_attn_kernel,
        grid=(BATCH, H // 2),
        in_specs=[blk(), blk(), blk()],
        out_specs=blk(),
        out_shape=jax.ShapeDtypeStruct((N_TOK, NA_W), BF16),
        compiler_params=_cparams(("arbitrary", "arbitrary"), 24),
        name="ctx_attn",
    )(q, k, v)


def _na_attn_kernel(q_ref, k_ref, v_ref, ck_ref, cv_ref, tab_ref, prev_ref, o_ref, kk_ref, vv_ref):
    del prev_ref
    for hh in range(2):
        sl = slice(hh * DH, (hh + 1) * DH)
        kk_ref[hh, 0:DEC_SEQ, :] = k_ref[:, sl].astype(BF16)
        kk_ref[hh, DEC_SEQ:DEC_SEQ + PAST, :] = ck_ref[0, 0, :, sl].astype(BF16)
        vv_ref[hh, 0:DEC_SEQ, :] = v_ref[:, sl].astype(BF16)
        vv_ref[hh, DEC_SEQ:DEC_SEQ + PAST, :] = cv_ref[0, 0, :, sl].astype(BF16)

    win = NA_ROWS * GRID_W

    def body(qr, carry):
        rs = jnp.clip(qr - NA_ROWS // 2, 0, DEC_SEQ // GRID_W - NA_ROWS)
        d = qr - rs
        q0 = pl.multiple_of(qr * GRID_W, GRID_W)
        k0 = pl.multiple_of(rs * GRID_W, GRID_W)
        outs = []
        for hh in range(2):
            qh = q_ref[pl.ds(q0, GRID_W), hh * DH:(hh + 1) * DH]
            s_lat = _dot_nt(qh, kk_ref[hh, pl.ds(k0, win), :]) + tab_ref[hh, d]
            s_ctx = _dot_nt(qh, kk_ref[hh, DEC_SEQ:DEC_SEQ + PAST, :])
            outs.append(_softmax_pv([s_lat, s_ctx],
                                    [vv_ref[hh, pl.ds(k0, win), :], vv_ref[hh, DEC_SEQ:DEC_SEQ + PAST, :]]))
        o_ref[pl.ds(q0, GRID_W), :] = jnp.concatenate(outs, axis=-1).astype(o_ref.dtype)
        return carry

    lax.fori_loop(0, DEC_SEQ // GRID_W, body, 0)


def _na_bias_table(rpb_l):
    d = jnp.arange(NA_ROWS)
    kr = jnp.arange(NA_ROWS)
    qc = jnp.arange(GRID_W)
    kc = jnp.arange(GRID_W)
    d_row = kr[None, :] - d[:, None] + (NA_ROWS - 1)
    d_col = jnp.clip(kc[None, :] - qc[:, None] + (NA_COLS - 1), 0, 2 * NA_COLS - 2)
    cs = jnp.clip(qc - NA_COLS // 2, 0, GRID_W - NA_COLS)
    in_col = (kc[None, :] >= cs[:, None]) & (kc[None, :] < cs[:, None] + NA_COLS)
    t = rpb_l[:, d_row[:, None, :, None], d_col[None, :, None, :]]
    t = jnp.where(in_col[None, None, :, None, :], t.astype(F32), NEG_INF)
    return t.reshape(H, NA_ROWS, GRID_W, NA_ROWS * GRID_W)


def _na_attention(q, k, v, cache_k, cache_v, tab, att_prev, layer):
    ctx_blocks = N_CTX // DEC_SEQ
    blk = lambda: pl.BlockSpec((DEC_SEQ, 2 * DH), lambda g, b: (ctx_blocks + b, g))
    cache = lambda: pl.BlockSpec((1, 1, PAST, 2 * DH), lambda g, b: (b, layer, 0, g))
    return pl.pallas_call(
        _na_attn_kernel,
        grid=(H // 2, DEC_BATCH),
        in_specs=[blk(), blk(), blk(), cache(), cache(),
                  pl.BlockSpec((2, NA_ROWS, GRID_W, NA_ROWS * GRID_W), lambda g, b: (g, 0, 0, 0)),
                  pl.BlockSpec(memory_space=pl.ANY)],
        out_specs=blk(),
        out_shape=jax.ShapeDtypeStruct((N_TOK, NA_W), BF16),
        scratch_shapes=[pltpu.VMEM((2, DEC_SEQ + PAST, DH), BF16),
                        pltpu.VMEM((2, DEC_SEQ + PAST, DH), BF16)],
        input_output_aliases={6: 0},
        compiler_params=_cparams(("arbitrary", "arbitrary"), 32),
        name="na_attn",
    )(q, k, v, cache_k, cache_v, tab, att_prev)


def _pool_kernel(u_ref, wp_ref, ps_ref, o_ref):
    n = jnp.where(pl.program_id(0) < CTX_UNITS, SEQ, DEC_SEQ)
    t = lax.broadcasted_iota(jnp.int32, (UNIT, PG), 0)
    p = jnp.bitwise_and(t, n - 1)

    def fwd(a, s):
        return jnp.where(p < n - s, pltpu.roll(a, UNIT - s, 0), 0.0)

    def bwd(a, s):
        return jnp.where(p >= s, pltpu.roll(a, s, 0), 0.0)

    for gi, w in enumerate(POOL_WINDOWS):
        half = w // 2
        cols = slice(gi * PG, (gi + 1) * PG)
        g = u_ref[:, cols]
        f, b, s = g, g, 1
        while s < half:
            f = f + fwd(f, s)
            b = b + bwd(b, s)
            s *= 2
        tot = f + bwd(b, 1)
        cnt = (jnp.minimum(p + half, n) - jnp.maximum(p - half, 0)).astype(F32)
        dlt = tot / cnt - g
        out = _dot(dlt.astype(BF16), wp_ref[gi].astype(BF16)) * ps_ref[:, cols]
        o_ref[:, cols] = out.astype(o_ref.dtype)


def _pool_mixer(u, w_pool, pool_scale, layer):
    return pl.pallas_call(
        _pool_kernel,
        grid=(N_UNITS,),
        in_specs=[pl.BlockSpec((UNIT, POOL_W), lambda i: (i, 0)),
                  pl.BlockSpec((None, len(POOL_WINDOWS), PG, PG), lambda i: (layer, 0, 0, 0)),
                  pl.BlockSpec((None, 1, POOL_W), lambda i: (layer, 0, 0))],
        out_specs=pl.BlockSpec((UNIT, POOL_W), lambda i: (i, 0)),
        out_shape=jax.ShapeDtypeStruct((N_TOK, POOL_W), BF16),
        compiler_params=_cparams(("arbitrary",), 40),
        name="pool",
    )(u, w_pool, pool_scale.reshape(DEPTH, 1, POOL_W))


def _outproj_kernel(att_ref, pool_ref, x_ref, w_ref, gate_ref, g2_ref, sh_ref, sc_ref, wr_ref,
                    y_ref, h_ref, lg_ref, wb_ref):
    @pl.when(pl.program_id(0) == 0)
    def _():
        wb_ref[...] = w_ref[...].astype(BF16)

    mix = _dot(att_ref[...], wb_ref[0:NA_W, :]) + _dot(pool_ref[...], wb_ref[NA_W:NA_W + POOL_W, :])
    y = x_ref[...] + gate_ref[0] * mix
    y_ref[...] = y
    h = _norm_mod(y, g2_ref[0], sh_ref[0], sc_ref[0]).astype(BF16)
    h_ref[...] = h
    lg_ref[...] = _dot(h, wr_ref[...])


def _outproj(att, pool, x, mods, norm_gain, w_out, w_router_pad, layer):
    tm = 256
    row = lambda shape: pl.BlockSpec(shape, lambda i: (i, 0))
    return pl.pallas_call(
        _outproj_kernel,
        grid=(N_TOK // tm,),
        in_specs=[row((tm, NA_W)), row((tm, POOL_W)), row((tm, D)),
                  _resident((None, D, D), lambda i: (layer, 0, 0)),
                  _mod_spec(layer, 2, tm),
                  pl.BlockSpec((1, 1, D), lambda i: (layer, 0, 0)),
                  _mod_spec(layer, 3, tm),
                  _mod_spec(layer, 4, tm),
                  _resident((D, LANES), lambda i: (0, 0))],
        out_specs=[row((tm, D)), row((tm, D)), row((tm, LANES))],
        out_shape=[jax.ShapeDtypeStruct((N_TOK, D), F32),
                   jax.ShapeDtypeStruct((N_TOK, D), BF16),
                   jax.ShapeDtypeStruct((N_TOK, LANES), F32)],
        scratch_shapes=[pltpu.VMEM((D, D), BF16)],
        compiler_params=_cparams(("arbitrary",), 52),
        name="outproj",
    )(att, pool, x, w_out, mods, norm_gain, mods, mods, w_router_pad)


def _route_kernel(lg_ref, pos_ref, aff_ref, *, n_req, n_tok, cap):
    lg = lg_ref[...].T[0:E, :]
    ex = jnp.exp(lg - jnp.max(lg, axis=0, keepdims=True))
    aff = ex / jnp.sum(ex, axis=0, keepdims=True)
    aff_ref[0] = aff

    rows = jnp.concatenate([aff[:, r * n_tok:(r + 1) * n_tok] for r in range(n_req)], axis=0)
    thr = jnp.zeros((n_req * E, 1), jnp.int32)
    for b in range(30, -1, -1):
        cand = thr | (1 << b)
        cnt = jnp.sum((rows >= lax.bitcast_convert_type(cand, F32)).astype(F32), axis=-1, keepdims=True)
        thr = jnp.where(cnt >= cap, cand, thr)
    thr_val = lax.bitcast_convert_type(thr, F32)
    gt = rows > thr_val
    eq = rows == thr_val
    n_gt = jnp.sum(gt.astype(F32), axis=-1, keepdims=True)
    before = (lax.broadcasted_iota(jnp.int32, (n_tok, n_tok), 0)
              < lax.broadcasted_iota(jnp.int32, (n_tok, n_tok), 1)).astype(BF16)
    eq_rank = _dot(eq.astype(BF16), before)
    sel = gt | (eq & (eq_rank < cap - n_gt))
    slot = _dot(sel.astype(BF16), before)
    pieces = [jnp.where(sel[r * E:(r + 1) * E], slot[r * E:(r + 1) * E] + r * cap, -1.0) for r in range(n_req)]
    pos_ref[0] = jnp.concatenate(pieces, axis=1) if n_req > 1 else pieces[0]


def _route(logits, pos_prev, aff_prev, first_unit, n_units, n_req, n_tok, cap):
    out_blk = lambda: pl.BlockSpec((1, E, UNIT), lambda i: (first_unit + i, 0, 0))
    in_specs = [pl.BlockSpec((UNIT, LANES), lambda i: (first_unit + i, 0))]
    args = [logits]
    aliases = {}
    if pos_prev is not None:
        in_specs += [pl.BlockSpec(memory_space=pl.ANY), pl.BlockSpec(memory_space=pl.ANY)]
        args += [pos_prev, aff_prev]
        aliases = {1: 0, 2: 1}

    def kern(lg_ref, *rest):
        _route_kernel(lg_ref, rest[-2], rest[-1], n_req=n_req, n_tok=n_tok, cap=cap)

    return pl.pallas_call(
        kern,
        grid=(n_units,),
        in_specs=in_specs,
        out_specs=[out_blk(), out_blk()],
        out_shape=[jax.ShapeDtypeStruct((N_UNITS, E, UNIT), F32)] * 2,
        input_output_aliases=aliases,
        compiler_params=_cparams(("arbitrary",), 32),
        name=f"route{n_tok}",
    )(*args)


def _onehot_rows(pos, p_ref, width):
    slot_id = lax.broadcasted_iota(jnp.int32, (SLOTS, width), 0).astype(F32)
    matches = []
    for e in range(E):
        match = pos[e:e + 1, :] == slot_id
        p_ref[e * SLOTS:(e + 1) * SLOTS, :] = jnp.where(match, 1.0, 0.0).astype(BF16)
        matches.append(match)
    return matches


def _gather_kernel(pos_ref, aff_ref, h_ref, xs_ref, gs_ref, p_ref):
    aff = aff_ref[0]
    matches = _onehot_rows(pos_ref[0], p_ref, UNIT)
    for e in range(E):
        gs_ref[e] = jnp.sum(jnp.where(matches[e], aff[e:e + 1, :], 0.0), axis=-1, keepdims=True)
    tn = 512
    for c in range(D // tn):
        r = _dot(p_ref[...], h_ref[:, c * tn:(c + 1) * tn])
        xs_ref[:, :, c * tn:(c + 1) * tn] = r.reshape(E, SLOTS, tn).astype(BF16)


def _gather(pos, aff, h):
    unit3 = lambda: pl.BlockSpec((1, E, UNIT), lambda u: (u, 0, 0))
    return pl.pallas_call(
        _gather_kernel,
        grid=(N_UNITS,),
        in_specs=[unit3(), unit3(), pl.BlockSpec((UNIT, D), lambda u: (u, 0))],
        out_specs=[pl.BlockSpec((E, SLOTS, D), lambda u: (0, u, 0)),
                   pl.BlockSpec((E, SLOTS, 1), lambda u: (0, u, 0))],
        out_shape=[jax.ShapeDtypeStruct((E, N_SLOT, D), BF16),
                   jax.ShapeDtypeStruct((E, N_SLOT, 1), F32)],
        scratch_shapes=[pltpu.VMEM((E * SLOTS, UNIT), BF16)],
        compiler_params=_cparams(("arbitrary",), 48),
        name="gather",
    )(pos, aff, h)


_FFN_ROWS = 256


def _ffn_up_kernel(xs_ref, wg_ref, wu_ref, o_ref):
    wg = wg_ref[...].astype(BF16)
    wu = wu_ref[...].astype(BF16)
    for m in range(N_SLOT // _FFN_ROWS):
        rows = slice(m * _FFN_ROWS, (m + 1) * _FFN_ROWS)
        x = xs_ref[rows, :]
        o_ref[rows, :] = (_silu(_dot(x, wg)) * _dot(x, wu)).astype(o_ref.dtype)


def _ffn_up(xs, w_gate, w_up, layer):
    tf = 512
    wspec = lambda: pl.BlockSpec((None, None, D, tf), lambda e, j: (layer, e, 0, j))
    return pl.pallas_call(
        _ffn_up_kernel,
        grid=(E, D_EXP // tf),
        in_specs=[pl.BlockSpec((None, N_SLOT, D), lambda e, j: (e, 0, 0)), wspec(), wspec()],
        out_specs=pl.BlockSpec((None, N_SLOT, tf), lambda e, j: (e, 0, j)),
        out_shape=jax.ShapeDtypeStruct((E, N_SLOT, D_EXP), BF16),
        compiler_params=_cparams(("arbitrary", "arbitrary"), 48),
        name="ffn_up",
    )(xs, w_gate, w_up)


def _ffn_down_kernel(h_ref, wd_ref, gs_ref, o_ref):
    wd = wd_ref[...].astype(BF16)
    for m in range(N_SLOT // _FFN_ROWS):
        rows = slice(m * _FFN_ROWS, (m + 1) * _FFN_ROWS)
        o_ref[rows, :] = (_dot(h_ref[rows, :], wd) * gs_ref[rows, :]).astype(o_ref.dtype)


def _ffn_down(hcur, w_down, gslot, layer):
    tn = 512
    return pl.pallas_call(
        _ffn_down_kernel,
        grid=(E, D // tn),
        in_specs=[pl.BlockSpec((None, N_SLOT, D_EXP), lambda e, j: (e, 0, 0)),
                  pl.BlockSpec((None, None, D_EXP, tn), lambda e, j: (layer, e, 0, j)),
                  pl.BlockSpec((None, N_SLOT, 1), lambda e, j: (e, 0, 0))],
        out_specs=pl.BlockSpec((None, N_SLOT, tn), lambda e, j: (e, 0, j)),
        out_shape=jax.ShapeDtypeStruct((E, N_SLOT, D), BF16),
        compiler_params=_cparams(("arbitrary", "arbitrary"), 32),
        name="ffn_down",
    )(hcur, w_down, gslot)


_SC_TOK = 256


def _scatter_kernel(pos_ref, ys_ref, y_ref, gate_ref, o_ref, p_ref):
    _onehot_rows(pos_ref[0], p_ref, _SC_TOK)
    tn = 512
    for c in range(D // tn):
        cols = slice(c * tn, (c + 1) * tn)
        ys = ys_ref[:, :, cols].reshape(E * SLOTS, tn)
        o_ref[:, cols] = y_ref[:, cols] + gate_ref[0][:, cols] * _dot_tn(p_ref[...], ys)


def _scatter(pos, ys, y, mods, layer):
    per_unit = UNIT // _SC_TOK
    tok = lambda: pl.BlockSpec((_SC_TOK, D), lambda u, s: (u * per_unit + s, 0))
    return pl.pallas_call(
        _scatter_kernel,
        grid=(N_UNITS, per_unit),
        in_specs=[pl.BlockSpec((1, E, _SC_TOK), lambda u, s: (u, 0, s)),
                  pl.BlockSpec((E, SLOTS, D), lambda u, s: (0, u, 0)),
                  tok(),
                  _mod_spec(layer, 5, UNIT, axis=0)],
        out_specs=tok(),
        out_shape=jax.ShapeDtypeStruct((N_TOK, D), F32),
        scratch_shapes=[pltpu.VMEM((E * SLOTS, _SC_TOK), BF16)],
        compiler_params=_cparams(("arbitrary", "arbitrary"), 40),
        name="scatter",
    )(pos, ys, y, mods)


def _final_kernel(y_ref, g_ref, oc_ref, ol_ref, *, n_ctx_tiles):
    y = y_ref[...]
    r = y * lax.rsqrt(jnp.mean(y * y, axis=-1, keepdims=True) + RMS_EPS) * g_ref[...]
    i = pl.program_id(0)

    @pl.when(i < n_ctx_tiles)
    def _():
        oc_ref[...] = r

    @pl.when(i >= n_ctx_tiles)
    def _():
        ol_ref[...] = r


def _final_norm(y, gain):
    tm = 512
    n_ctx_tiles = N_CTX // tm
    return pl.pallas_call(
        functools.partial(_final_kernel, n_ctx_tiles=n_ctx_tiles),
        grid=(N_TOK // tm,),
        in_specs=[pl.BlockSpec((tm, D), lambda i: (i, 0)), pl.BlockSpec((1, D), lambda i: (0, 0))],
        out_specs=[pl.BlockSpec((tm, D), lambda i: (jnp.minimum(i, n_ctx_tiles - 1), 0)),
                   pl.BlockSpec((tm, D), lambda i: (jnp.maximum(i - n_ctx_tiles, 0), 0))],
        out_shape=[jax.ShapeDtypeStruct((N_CTX, D), F32), jax.ShapeDtypeStruct((N_LAT, D), F32)],
        compiler_params=_cparams(("arbitrary",), 32),
        name="final_norm",
    )(y, gain.reshape(1, D))


def kernel(x_prompt, x_sample, cache_k, cache_v, c, c_ctx, w_mod, b_mod, norm_mix, norm_ffn, w_in, rpb,
           w_pool, pool_scale, w_out, w_router, w_gate, w_up, w_down, norm_final):
    y = jnp.concatenate([x_prompt.reshape(N_CTX, D), x_sample.reshape(N_LAT, D)], axis=0)
    cond = jnp.zeros((N_ROW, D), F32).at[0].set(c_ctx).at[1:1 + DEC_BATCH].set(c)
    mods = _adaln(cond, w_mod, b_mod).reshape(DEPTH * N_ROW * N_MOD, 1, D)
    gain_mix = norm_mix.reshape(DEPTH, 1, D)
    gain_ffn = norm_ffn.reshape(DEPTH, 1, D)
    ck = cache_k.reshape(DEC_BATCH, DEPTH, PAST, NA_W)
    cv = cache_v.reshape(DEC_BATCH, DEPTH, PAST, NA_W)

    new_k, new_v = [], []
    for l in range(DEPTH):
        q = _inproj(y, mods, gain_mix, w_in, l, 0, BF16, out_scale=DH ** -0.5)
        k = _inproj(y, mods, gain_mix, w_in, l, 1, F32)
        v = _inproj(y, mods, gain_mix, w_in, l, 2, F32)
        u = _inproj(y, mods, gain_mix, w_in, l, 3, F32)
        new_k.append(k[:N_CTX].reshape(BATCH, SEQ, H, DH))
        new_v.append(v[:N_CTX].reshape(BATCH, SEQ, H, DH))

        att = _ctx_attention(q, k, v)
        att = _na_attention(q, k, v, ck, cv, _na_bias_table(rpb[l]), att, l)
        pool = _pool_mixer(u, w_pool, pool_scale, l)

        w_router_pad = jnp.pad(w_router[l], ((0, 0), (0, LANES - E))).astype(BF16)
        y, h, logits = _outproj(att, pool, y, mods, gain_ffn, w_out, w_router_pad, l)

        pos, aff = _route(logits, None, None, 0, CTX_UNITS, UNIT // SEQ, SEQ, CAP_CTX)
        pos, aff = _route(logits, pos, aff, CTX_UNITS, N_UNITS - CTX_UNITS, 1, DEC_SEQ, CAP_LAT)
        xs, gslot = _gather(pos, aff, h)
        ys = _ffn_down(_ffn_up(xs, w_gate, w_up, l), w_down, gslot, l)
        y = _scatter(pos, ys, y, mods, l)

    y_prompt, y_sample = _final_norm(y, norm_final)
    return (y_prompt.reshape(BATCH, SEQ, D), y_sample.reshape(DEC_BATCH, DEC_SEQ, D),
            jnp.stack(new_k, axis=1), jnp.stack(new_v, axis=1))
```

```python
import functools

import jax
import jax.numpy as jnp
from jax import lax
from jax.experimental import pallas as pl
from jax.experimental.pallas import tpu as pltpu

F32 = jnp.float32
BF16 = jnp.bfloat16

D = 2048
BATCH, SEQ = 16, 256
DEPTH = 2
DEC_BATCH, DEC_SEQ = 8, 1024
PAST = 512
GRID_W = 64
GRID_H = DEC_SEQ // GRID_W
H, DH = 16, 64
NA_W = H * DH
POOL_WINDOWS = (2, 4, 8, 16)
PG = 256
POOL_W = 1024
IN_W = 3 * NA_W + POOL_W
NA_ROWS, NA_COLS = 8, 16
E = 16
D_EXP = 1024
N_MOD = 6
RMS_EPS = 1e-6
NEG_INF = -1e30

N_CTX = BATCH * SEQ
N_LAT = DEC_BATCH * DEC_SEQ
N_TOK = N_CTX + N_LAT
UNIT = 1024
N_UNITS = N_TOK // UNIT
CTX_UNITS = N_CTX // UNIT
REQ_PER_CTX_UNIT = UNIT // SEQ
SLOTS = 128
CAP_CTX = 2 * SEQ // E
CAP_LAT = 2 * DEC_SEQ // E
N_SLOT = N_UNITS * SLOTS
N_ROW = 16
LANES = 128
MIB = 1024 * 1024


def _cparams(sem, vmem_mib):
    return pltpu.CompilerParams(dimension_semantics=sem, vmem_limit_bytes=vmem_mib * MIB)


def _resident(block_shape, index_map):
    return pl.BlockSpec(block_shape, index_map, pipeline_mode=pl.Buffered(1))


def _dot(a, b):
    return jnp.dot(a, b, preferred_element_type=F32)


def _dot_nt(a, b):
    return lax.dot_general(a, b, (((1,), (1,)), ((), ())), preferred_element_type=F32)


def _dot_tn(a, b):
    return lax.dot_general(a, b, (((0,), (0,)), ((), ())), preferred_element_type=F32)


def _silu(x):
    return x / (1.0 + jnp.exp(-x))


def _norm_mod(x, gain, shift, scale):
    y = x * lax.rsqrt(jnp.mean(x * x, axis=-1, keepdims=True) + RMS_EPS)
    return (y * gain) * (1.0 + scale) + shift


def _mod_row_of_tile(i, tile):
    per_req = DEC_SEQ // tile
    n_ctx_tiles = N_CTX // tile
    return jnp.where(i < n_ctx_tiles, 0, 1 + (i - n_ctx_tiles) // per_req)


def _mod_spec(layer, which, tile, axis=0):
    def imap(*ids):
        return ((layer * N_ROW + _mod_row_of_tile(ids[axis], tile)) * N_MOD + which, 0, 0)
    return pl.BlockSpec((1, 1, D), imap)


def _token_specs(srcs, tile, width):
    if len(srcs) == 1:
        return [pl.BlockSpec((tile, width), lambda i: (i, 0))]
    n_ctx_tiles = N_CTX // tile
    return [pl.BlockSpec((tile, width), lambda i: (jnp.minimum(i, n_ctx_tiles - 1), 0)),
            pl.BlockSpec((tile, width), lambda i: (jnp.maximum(i - n_ctx_tiles, 0), 0))]


def _read_tokens(refs, tile):
    if len(refs) == 1:
        return refs[0][...]
    return jnp.where(pl.program_id(0) < N_CTX // tile, refs[0][...], refs[1][...])


def _adaln_kernel(c_ref, w_ref, b_ref, o_ref):
    a = _silu(c_ref[...]).astype(BF16)
    o_ref[0] = _dot(a, w_ref[0].astype(BF16)) + b_ref[0]


def _adaln(cond, w_mod, b_mod):
    tn = 1024
    return pl.pallas_call(
        _adaln_kernel,
        grid=(DEPTH, N_MOD * D // tn),
        in_specs=[pl.BlockSpec((N_ROW, D), lambda l, j: (0, 0)),
                  pl.BlockSpec((1, D, tn), lambda l, j: (l, 0, j)),
                  pl.BlockSpec((1, 1, tn), lambda l, j: (l, 0, j))],
        out_specs=pl.BlockSpec((1, N_ROW, tn), lambda l, j: (l, 0, j)),
        out_shape=jax.ShapeDtypeStruct((DEPTH, N_ROW, N_MOD * D), F32),
        compiler_params=_cparams(("arbitrary", "arbitrary"), 40),
        name="adaln",
    )(cond, w_mod, b_mod.reshape(DEPTH, 1, N_MOD * D))


def _cast_kernel(w_ref, o_ref):
    o_ref[...] = w_ref[...].astype(BF16)


def _to_bf16(w, layer):
    rows, cols = w.shape[1:]
    tr = 256
    return pl.pallas_call(
        _cast_kernel,
        grid=(rows // tr,),
        in_specs=[pl.BlockSpec((None, tr, cols), lambda i: (layer, i, 0))],
        out_specs=pl.BlockSpec((tr, cols), lambda i: (i, 0)),
        out_shape=jax.ShapeDtypeStruct((rows, cols), BF16),
        compiler_params=_cparams(("arbitrary",), 32),
        name="cast_bf16",
    )(w)


_INPROJ_TILE = 256


def _inproj_kernel(*refs, n_src):
    x_refs = refs[:n_src]
    g_ref, sh_ref, sc_ref, w_ref, q_ref, k_ref, v_ref, u_ref, kf_ref, vf_ref = refs[n_src:]
    h = _norm_mod(_read_tokens(x_refs, _INPROJ_TILE), g_ref[0], sh_ref[0], sc_ref[0]).astype(BF16)
    q_ref[...] = (_dot(h, w_ref[:, 0:NA_W]) * (DH ** -0.5)).astype(BF16)
    k = _dot(h, w_ref[:, NA_W:2 * NA_W])
    v = _dot(h, w_ref[:, 2 * NA_W:3 * NA_W])
    k_ref[...] = k.astype(BF16)
    v_ref[...] = v.astype(BF16)
    u_ref[...] = _dot(h, w_ref[:, 3 * NA_W:IN_W])

    @pl.when(pl.program_id(0) < N_CTX // _INPROJ_TILE)
    def _():
        kf_ref[...] = k
        vf_ref[...] = v


def _inproj(srcs, mods, norm_gain, w_in_bf, layer):
    tm = _INPROJ_TILE
    n_ctx_tiles = N_CTX // tm
    tok = lambda: pl.BlockSpec((tm, NA_W), lambda i: (i, 0))
    ctx_only = lambda: pl.BlockSpec((tm, NA_W), lambda i: (jnp.minimum(i, n_ctx_tiles - 1), 0))
    return pl.pallas_call(
        functools.partial(_inproj_kernel, n_src=len(srcs)),
        grid=(N_TOK // tm,),
        in_specs=_token_specs(srcs, tm, D) + [
            pl.BlockSpec((1, 1, D), lambda i: (layer, 0, 0)),
            _mod_spec(layer, 0, tm),
            _mod_spec(layer, 1, tm),
            _resident((D, IN_W), lambda i: (0, 0))],
        out_specs=[tok(), tok(), tok(), tok(), ctx_only(), ctx_only()],
        out_shape=[jax.ShapeDtypeStruct((N_TOK, NA_W), BF16)] * 3
        + [jax.ShapeDtypeStruct((N_TOK, POOL_W), F32)]
        + [jax.ShapeDtypeStruct((N_CTX, NA_W), F32)] * 2,
        compiler_params=_cparams(("arbitrary",), 48),
        name="inproj",
    )(*srcs, norm_gain, mods, mods, w_in_bf)


def _na_window_start(qr):
    return min(max(qr - NA_ROWS // 2, 0), GRID_H - NA_ROWS)


def _na_bias_kernel(rpb_ref, o_ref):
    qc = lax.broadcasted_iota(jnp.int32, (GRID_W, LANES), 0)
    kc = lax.broadcasted_iota(jnp.int32, (GRID_W, LANES), 1)
    cs = jnp.clip(qc - NA_COLS // 2, 0, GRID_W - NA_COLS)
    in_col = (kc >= cs) & (kc < cs + NA_COLS)
    by_row_offset = []
    for dr in range(2 * NA_ROWS - 1):
        row = jnp.broadcast_to(rpb_ref[0, dr:dr + 1, :], (GRID_W, LANES))
        toep = pltpu.roll(row, LANES - (NA_COLS - 1), 1, stride=1, stride_axis=0)
        by_row_offset.append(jnp.where(in_col, toep, NEG_INF)[:, 0:GRID_W])
    masked = jnp.full((GRID_W, GRID_W), NEG_INF, F32)
    for qr in range(GRID_H):
        rs = _na_window_start(qr)
        for kr in range(GRID_H):
            blk = by_row_offset[kr - qr + NA_ROWS - 1] if rs <= kr < rs + NA_ROWS else masked
            o_ref[0, qr * GRID_W:(qr + 1) * GRID_W, kr * GRID_W:(kr + 1) * GRID_W] = blk


def _na_bias_table(rpb_l):
    rpb_pad = jnp.pad(rpb_l, ((0, 0), (0, 0), (0, LANES - rpb_l.shape[-1])))
    n_dr = rpb_l.shape[1]
    return pl.pallas_call(
        _na_bias_kernel,
        grid=(H,),
        in_specs=[pl.BlockSpec((1, n_dr, LANES), lambda h: (h, 0, 0))],
        out_specs=pl.BlockSpec((1, DEC_SEQ, DEC_SEQ), lambda h: (h, 0, 0)),
        out_shape=jax.ShapeDtypeStruct((H, DEC_SEQ, DEC_SEQ), F32),
        compiler_params=_cparams(("arbitrary",), 24),
        name="na_bias",
    )(rpb_pad)


_NA_Q_BLOCK = 4 * GRID_W


def _na_key_range(qb):
    rows_per_block = _NA_Q_BLOCK // GRID_W
    lo = _na_window_start(qb * rows_per_block) * GRID_W
    hi = (_na_window_start((qb + 1) * rows_per_block - 1) + NA_ROWS) * GRID_W
    return (lo // _NA_Q_BLOCK * _NA_Q_BLOCK, -(-hi // _NA_Q_BLOCK) * _NA_Q_BLOCK)


_NA_KEY_RANGES = [_na_key_range(qb) for qb in range(DEC_SEQ // _NA_Q_BLOCK)]


def _attn_kernel(q_ref, k_ref, v_ref, ck_ref, cv_ref, tab_ref, o_ref, kc_ref, vv_ref):
    lane = lax.broadcasted_iota(jnp.int32, (1, 2 * DH), 1)
    head_lanes = [lane < DH, lane >= DH]
    vv_ref[0:UNIT, 0:2 * DH] = v_ref[...]
    vv_ref[:, 2 * DH:4 * DH] = jnp.ones((UNIT + PAST, 2 * DH), BF16)

    def head_queries(rows):
        q = q_ref[rows, :]
        return [jnp.where(m, q, jnp.zeros_like(q)) for m in head_lanes]

    def probs(scores):
        m = functools.reduce(jnp.maximum, [jnp.max(s, axis=-1, keepdims=True) for s in scores])
        return [jnp.exp(s - m).astype(BF16) for s in scores]

    def merge_heads(acc):
        outs = [a[:, 0:2 * DH] / a[:, 2 * DH:4 * DH] for a in acc]
        return jnp.where(head_lanes[0], outs[0], outs[1]).astype(o_ref.dtype)

    @pl.when(pl.program_id(1) < CTX_UNITS)
    def _():
        for r in range(REQ_PER_CTX_UNIT):
            rows = slice(r * SEQ, (r + 1) * SEQ)
            acc = []
            for qh in head_queries(rows):
                (p,) = probs([_dot_nt(qh, k_ref[rows, :])])
                acc.append(_dot(p, vv_ref[rows, :]))
            o_ref[rows, :] = merge_heads(acc)

    @pl.when(pl.program_id(1) >= CTX_UNITS)
    def _():
        kc_ref[...] = ck_ref[0, 0].astype(BF16)
        vv_ref[UNIT:UNIT + PAST, 0:2 * DH] = cv_ref[0, 0].astype(BF16)
        for qb, (k_lo, k_hi) in enumerate(_NA_KEY_RANGES):
            rows = slice(qb * _NA_Q_BLOCK, (qb + 1) * _NA_Q_BLOCK)
            keys = slice(k_lo, k_hi)
            acc = []
            for hh, qh in enumerate(head_queries(rows)):
                s_lat = _dot_nt(qh, k_ref[keys, :]) + tab_ref[hh, rows, keys]
                s_ctx = _dot_nt(qh, kc_ref[...])
                p_lat, p_ctx = probs([s_lat, s_ctx])
                acc.append(_dot(p_lat, vv_ref[keys, :]) + _dot(p_ctx, vv_ref[UNIT:UNIT + PAST, :]))
            o_ref[rows, :] = merge_heads(acc)


def _attention(q, k, v, cache_k, cache_v, tab, layer):
    blk = lambda: pl.BlockSpec((UNIT, 2 * DH), lambda g, u: (u, g))
    cache = lambda: pl.BlockSpec((1, 1, PAST, 2 * DH), lambda g, u: (jnp.maximum(u - CTX_UNITS, 0), layer, 0, g))
    return pl.pallas_call(
        _attn_kernel,
        grid=(H // 2, N_UNITS),
        in_specs=[blk(), blk(), blk(), cache(), cache(),
                  pl.BlockSpec((2, DEC_SEQ, DEC_SEQ), lambda g, u: (g, 0, 0))],
        out_specs=blk(),
        out_shape=jax.ShapeDtypeStruct((N_TOK, NA_W), BF16),
        scratch_shapes=[pltpu.VMEM((PAST, 2 * DH), BF16),
                        pltpu.VMEM((UNIT + PAST, 4 * DH), BF16)],
        compiler_params=_cparams(("arbitrary", "arbitrary"), 44),
        name="attention",
    )(q, k, v, cache_k, cache_v, tab)


def _pool_kernel(u_ref, wp_ref, ps_ref, o_ref):
    n = jnp.where(pl.program_id(0) < CTX_UNITS, SEQ, DEC_SEQ)
    t = lax.broadcasted_iota(jnp.int32, (UNIT, PG), 0)
    p = jnp.bitwise_and(t, n - 1)

    def fwd(a, s):
        return jnp.where(p < n - s, pltpu.roll(a, UNIT - s, 0), 0.0)

    def bwd(a, s):
        return jnp.where(p >= s, pltpu.roll(a, s, 0), 0.0)

    for gi, w in enumerate(POOL_WINDOWS):
        half = w // 2
        cols = slice(gi * PG, (gi + 1) * PG)
        g = u_ref[:, cols]
        f, b, s = g, g, 1
        while s < half:
            f = f + fwd(f, s)
            b = b + bwd(b, s)
            s *= 2
        tot = f + bwd(b, 1)
        cnt = (jnp.minimum(p + half, n) - jnp.maximum(p - half, 0)).astype(F32)
        dlt = tot / cnt - g
        out = _dot(dlt.astype(BF16), wp_ref[gi].astype(BF16)) * ps_ref[:, cols]
        o_ref[:, cols] = out.astype(o_ref.dtype)


def _pool_mixer(u, w_pool, pool_scale, layer):
    return pl.pallas_call(
        _pool_kernel,
        grid=(N_UNITS,),
        in_specs=[pl.BlockSpec((UNIT, POOL_W), lambda i: (i, 0)),
                  pl.BlockSpec((None, len(POOL_WINDOWS), PG, PG), lambda i: (layer, 0, 0, 0)),
                  pl.BlockSpec((None, 1, POOL_W), lambda i: (layer, 0, 0))],
        out_specs=pl.BlockSpec((UNIT, POOL_W), lambda i: (i, 0)),
        out_shape=jax.ShapeDtypeStruct((N_TOK, POOL_W), BF16),
        compiler_params=_cparams(("arbitrary",), 40),
        name="pool",
    )(u, w_pool, pool_scale.reshape(DEPTH, 1, POOL_W))


_OUTPROJ_TILE = 512


def _outproj_kernel(*refs, n_src):
    x_refs = refs[:n_src]
    att_ref, pool_ref, w_ref, gate_ref, g2_ref, sh_ref, sc_ref, wr_ref, y_ref, h_ref, lg_ref = refs[n_src:]
    mix = _dot(att_ref[...], w_ref[0:NA_W, :]) + _dot(pool_ref[...], w_ref[NA_W:NA_W + POOL_W, :])
    y = _read_tokens(x_refs, _OUTPROJ_TILE) + gate_ref[0] * mix
    y_ref[...] = y
    h = _norm_mod(y, g2_ref[0], sh_ref[0], sc_ref[0]).astype(BF16)
    h_ref[...] = h
    lg_ref[...] = _dot(h, wr_ref[...])


def _outproj(srcs, att, pool, mods, norm_gain, w_out_bf, w_router_pad, layer):
    tm = _OUTPROJ_TILE
    row = lambda width: pl.BlockSpec((tm, width), lambda i: (i, 0))
    return pl.pallas_call(
        functools.partial(_outproj_kernel, n_src=len(srcs)),
        grid=(N_TOK // tm,),
        in_specs=_token_specs(srcs, tm, D) + [
            row(NA_W), row(POOL_W),
            _resident((D, D), lambda i: (0, 0)),
            _mod_spec(layer, 2, tm),
            pl.BlockSpec((1, 1, D), lambda i: (layer, 0, 0)),
            _mod_spec(layer, 3, tm),
            _mod_spec(layer, 4, tm),
            _resident((D, LANES), lambda i: (0, 0))],
        out_specs=[row(D), row(D), row(LANES)],
        out_shape=[jax.ShapeDtypeStruct((N_TOK, D), F32),
                   jax.ShapeDtypeStruct((N_TOK, D), BF16),
                   jax.ShapeDtypeStruct((N_TOK, LANES), F32)],
        compiler_params=_cparams(("arbitrary",), 52),
        name="outproj",
    )(*srcs, att, pool, w_out_bf, mods, norm_gain, mods, mods, w_router_pad)


def _choose_slots(aff, n_req, n_tok, cap):
    rows = jnp.concatenate([aff[:, r * n_tok:(r + 1) * n_tok] for r in range(n_req)], axis=0)
    thr = jnp.zeros((n_req * E, 1), jnp.int32)
    for b in range(30, -1, -1):
        cand = thr | (1 << b)
        cnt = jnp.sum((rows >= lax.bitcast_convert_type(cand, F32)).astype(F32), axis=-1, keepdims=True)
        thr = jnp.where(cnt >= cap, cand, thr)
    thr_val = lax.bitcast_convert_type(thr, F32)
    gt = rows > thr_val
    eq = rows == thr_val
    n_gt = jnp.sum(gt.astype(F32), axis=-1, keepdims=True)
    before = (lax.broadcasted_iota(jnp.int32, (n_tok, n_tok), 0)
              < lax.broadcasted_iota(jnp.int32, (n_tok, n_tok), 1)).astype(BF16)
    eq_rank = _dot(eq.astype(BF16), before)
    sel = gt | (eq & (eq_rank < cap - n_gt))
    slot = _dot(sel.astype(BF16), before)
    pieces = [jnp.where(sel[r * E:(r + 1) * E], slot[r * E:(r + 1) * E] + r * cap, -1.0) for r in range(n_req)]
    return jnp.concatenate(pieces, axis=1) if n_req > 1 else pieces[0]


def _route_kernel(lg_ref, pos_ref, aff_ref):
    lg = lg_ref[...].T[0:E, :]
    ex = jnp.exp(lg - jnp.max(lg, axis=0, keepdims=True))
    aff = ex / jnp.sum(ex, axis=0, keepdims=True)
    aff_ref[0] = aff

    @pl.when(pl.program_id(0) < CTX_UNITS)
    def _():
        pos_ref[0] = _choose_slots(aff, REQ_PER_CTX_UNIT, SEQ, CAP_CTX)

    @pl.when(pl.program_id(0) >= CTX_UNITS)
    def _():
        pos_ref[0] = _choose_slots(aff, 1, DEC_SEQ, CAP_LAT)


def _route(logits):
    out_blk = lambda: pl.BlockSpec((1, E, UNIT), lambda u: (u, 0, 0))
    return pl.pallas_call(
        _route_kernel,
        grid=(N_UNITS,),
        in_specs=[pl.BlockSpec((UNIT, LANES), lambda u: (u, 0))],
        out_specs=[out_blk(), out_blk()],
        out_shape=[jax.ShapeDtypeStruct((N_UNITS, E, UNIT), F32)] * 2,
        compiler_params=_cparams(("arbitrary",), 32),
        name="route",
    )(logits)


def _onehot_rows(pos, p_ref, width):
    slot_id = lax.broadcasted_iota(jnp.int32, (SLOTS, width), 0).astype(F32)
    matches = []
    for e in range(E):
        match = pos[e:e + 1, :] == slot_id
        p_ref[e * SLOTS:(e + 1) * SLOTS, :] = jnp.where(match, 1.0, 0.0).astype(BF16)
        matches.append(match)
    return matches


def _gather_kernel(pos_ref, aff_ref, h_ref, xs_ref, gs_ref, p_ref):
    aff = aff_ref[0]
    matches = _onehot_rows(pos_ref[0], p_ref, UNIT)
    for e in range(E):
        gate = jnp.sum(jnp.where(matches[e], aff[e:e + 1, :], 0.0), axis=-1, keepdims=True)
        gs_ref[e] = jnp.broadcast_to(gate, (SLOTS, LANES))
    tn = 512
    for c in range(D // tn):
        r = _dot(p_ref[...], h_ref[:, c * tn:(c + 1) * tn])
        xs_ref[:, :, c * tn:(c + 1) * tn] = r.reshape(E, SLOTS, tn).astype(BF16)


def _gather(pos, aff, h):
    unit3 = lambda: pl.BlockSpec((1, E, UNIT), lambda u: (u, 0, 0))
    return pl.pallas_call(
        _gather_kernel,
        grid=(N_UNITS,),
        in_specs=[unit3(), unit3(), pl.BlockSpec((UNIT, D), lambda u: (u, 0))],
        out_specs=[pl.BlockSpec((E, SLOTS, D), lambda u: (0, u, 0)),
                   pl.BlockSpec((E, SLOTS, LANES), lambda u: (0, u, 0))],
        out_shape=[jax.ShapeDtypeStruct((E, N_SLOT, D), BF16),
                   jax.ShapeDtypeStruct((E, N_SLOT, LANES), F32)],
        scratch_shapes=[pltpu.VMEM((E * SLOTS, UNIT), BF16)],
        compiler_params=_cparams(("arbitrary",), 48),
        name="gather",
    )(pos, aff, h)


_FFN_ROWS = 256
_FFN_COLS = 256


def _ffn_up_kernel(xs_ref, wg_ref, wu_ref, o_ref):
    for c in range(wg_ref.shape[1] // _FFN_COLS):
        cols = slice(c * _FFN_COLS, (c + 1) * _FFN_COLS)
        wg = wg_ref[:, cols].astype(BF16)
        wu = wu_ref[:, cols].astype(BF16)
        for m in range(N_SLOT // _FFN_ROWS):
            rows = slice(m * _FFN_ROWS, (m + 1) * _FFN_ROWS)
            x = xs_ref[rows, :]
            o_ref[rows, cols] = (_silu(_dot(x, wg)) * _dot(x, wu)).astype(o_ref.dtype)


def _ffn_up(xs, w_gate, w_up, layer):
    tf = 512
    wspec = lambda: pl.BlockSpec((None, None, D, tf), lambda e, j: (layer, e, 0, j))
    return pl.pallas_call(
        _ffn_up_kernel,
        grid=(E, D_EXP // tf),
        in_specs=[pl.BlockSpec((None, N_SLOT, D), lambda e, j: (e, 0, 0)), wspec(), wspec()],
        out_specs=pl.BlockSpec((None, N_SLOT, tf), lambda e, j: (e, 0, j)),
        out_shape=jax.ShapeDtypeStruct((E, N_SLOT, D_EXP), BF16),
        compiler_params=_cparams(("arbitrary", "arbitrary"), 48),
        name="ffn_up",
    )(xs, w_gate, w_up)


def _ffn_down_kernel(h_ref, wd_ref, gs_ref, o_ref):
    for c in range(wd_ref.shape[1] // _FFN_COLS):
        cols = slice(c * _FFN_COLS, (c + 1) * _FFN_COLS)
        wd = wd_ref[:, cols].astype(BF16)
        for m in range(N_SLOT // _FFN_ROWS):
            rows = slice(m * _FFN_ROWS, (m + 1) * _FFN_ROWS)
            gate = jnp.concatenate([gs_ref[rows, :]] * (_FFN_COLS // LANES), axis=1)
            o_ref[rows, cols] = (_dot(h_ref[rows, :], wd) * gate).astype(o_ref.dtype)


def _ffn_down(hcur, w_down, gslot, layer):
    tn = 512
    return pl.pallas_call(
        _ffn_down_kernel,
        grid=(E, D // tn),
        in_specs=[pl.BlockSpec((None, N_SLOT, D_EXP), lambda e, j: (e, 0, 0)),
                  pl.BlockSpec((None, None, D_EXP, tn), lambda e, j: (layer, e, 0, j)),
                  pl.BlockSpec((None, N_SLOT, LANES), lambda e, j: (e, 0, 0))],
        out_specs=pl.BlockSpec((None, N_SLOT, tn), lambda e, j: (e, 0, j)),
        out_shape=jax.ShapeDtypeStruct((E, N_SLOT, D), BF16),
        compiler_params=_cparams(("arbitrary", "arbitrary"), 32),
        name="ffn_down",
    )(hcur, w_down, gslot)


_SC_TOK = 256


def _scatter_kernel(pos_ref, ys_ref, y_ref, gate_ref, o_ref, p_ref):
    _onehot_rows(pos_ref[0], p_ref, _SC_TOK)
    tn = 512
    for c in range(D // tn):
        cols = slice(c * tn, (c + 1) * tn)
        ys = ys_ref[:, :, cols].reshape(E * SLOTS, tn)
        o_ref[:, cols] = y_ref[:, cols] + gate_ref[0][:, cols] * _dot_tn(p_ref[...], ys)


def _scatter(pos, ys, y, mods, layer):
    per_unit = UNIT // _SC_TOK
    tok = lambda: pl.BlockSpec((_SC_TOK, D), lambda u, s: (u * per_unit + s, 0))
    return pl.pallas_call(
        _scatter_kernel,
        grid=(N_UNITS, per_unit),
        in_specs=[pl.BlockSpec((1, E, _SC_TOK), lambda u, s: (u, 0, s)),
                  pl.BlockSpec((E, SLOTS, D), lambda u, s: (0, u, 0)),
                  tok(),
                  _mod_spec(layer, 5, UNIT)],
        out_specs=tok(),
        out_shape=jax.ShapeDtypeStruct((N_TOK, D), F32),
        scratch_shapes=[pltpu.VMEM((E * SLOTS, _SC_TOK), BF16)],
        compiler_params=_cparams(("arbitrary", "arbitrary"), 40),
        name="scatter",
    )(pos, ys, y, mods)


def _final_kernel(y_ref, g_ref, oc_ref, ol_ref, *, n_ctx_tiles):
    y = y_ref[...]
    r = y * lax.rsqrt(jnp.mean(y * y, axis=-1, keepdims=True) + RMS_EPS) * g_ref[...]
    i = pl.program_id(0)

    @pl.when(i < n_ctx_tiles)
    def _():
        oc_ref[...] = r

    @pl.when(i >= n_ctx_tiles)
    def _():
        ol_ref[...] = r


def _final_norm(y, gain):
    tm = 512
    n_ctx_tiles = N_CTX // tm
    return pl.pallas_call(
        functools.partial(_final_kernel, n_ctx_tiles=n_ctx_tiles),
        grid=(N_TOK // tm,),
        in_specs=[pl.BlockSpec((tm, D), lambda i: (i, 0)), pl.BlockSpec((1, D), lambda i: (0, 0))],
        out_specs=[pl.BlockSpec((tm, D), lambda i: (jnp.minimum(i, n_ctx_tiles - 1), 0)),
                   pl.BlockSpec((tm, D), lambda i: (jnp.maximum(i - n_ctx_tiles, 0), 0))],
        out_shape=[jax.ShapeDtypeStruct((N_CTX, D), F32), jax.ShapeDtypeStruct((N_LAT, D), F32)],
        compiler_params=_cparams(("arbitrary",), 32),
        name="final_norm",
    )(y, gain.reshape(1, D))


def kernel(x_prompt, x_sample, cache_k, cache_v, c, c_ctx, w_mod, b_mod, norm_mix, norm_ffn, w_in, rpb,
           w_pool, pool_scale, w_out, w_router, w_gate, w_up, w_down, norm_final):
    srcs = [x_prompt.reshape(N_CTX, D), x_sample.reshape(N_LAT, D)]
    cond = jnp.zeros((N_ROW, D), F32).at[0].set(c_ctx).at[1:1 + DEC_BATCH].set(c)
    mods = _adaln(cond, w_mod, b_mod).reshape(DEPTH * N_ROW * N_MOD, 1, D)
    gain_mix = norm_mix.reshape(DEPTH, 1, D)
    gain_ffn = norm_ffn.reshape(DEPTH, 1, D)
    ck = cache_k.reshape(DEC_BATCH, DEPTH, PAST, NA_W)
    cv = cache_v.reshape(DEC_BATCH, DEPTH, PAST, NA_W)

    new_k, new_v = [], []
    for l in range(DEPTH):
        q, k, v, u, k_ctx, v_ctx = _inproj(srcs, mods, gain_mix, _to_bf16(w_in, l), l)
        new_k.append(k_ctx.reshape(BATCH, SEQ, H, DH))
        new_v.append(v_ctx.reshape(BATCH, SEQ, H, DH))

        att = _attention(q, k, v, ck, cv, _na_bias_table(rpb[l]), l)
        pool = _pool_mixer(u, w_pool, pool_scale, l)

        w_router_pad = jnp.pad(w_router[l], ((0, 0), (0, LANES - E))).astype(BF16)
        y, h, logits = _outproj(srcs, att, pool, mods, gain_ffn, _to_bf16(w_out, l), w_router_pad, l)

        pos, aff = _route(logits)
        xs, gslot = _gather(pos, aff, h)
        ys = _ffn_down(_ffn_up(xs, w_gate, w_up, l), w_down, gslot, l)
        srcs = [_scatter(pos, ys, y, mods, l)]

    y_prompt, y_sample = _final_norm(srcs[0], norm_final)
    return (y_prompt.reshape(BATCH, SEQ, D), y_sample.reshape(DEC_BATCH, DEC_SEQ, D),
            jnp.stack(new_k, axis=1), jnp.stack(new_v, axis=1))
```

```python
import functools

import jax
import jax.numpy as jnp
from jax import lax
from jax.experimental import pallas as pl
from jax.experimental.pallas import tpu as pltpu

F32 = jnp.float32
BF16 = jnp.bfloat16

D = 2048
BATCH, SEQ = 16, 256
DEPTH = 2
DEC_BATCH, DEC_SEQ = 8, 1024
PAST = 512
GRID_W = 64
GRID_H = DEC_SEQ // GRID_W
H, DH = 16, 64
NA_W = H * DH
POOL_WINDOWS = (2, 4, 8, 16)
PG = 256
POOL_W = 1024
IN_W = 3 * NA_W + POOL_W
NA_ROWS, NA_COLS = 8, 16
E = 16
D_EXP = 1024
N_MOD = 6
RMS_EPS = 1e-6
NEG_INF = -1e30

N_CTX = BATCH * SEQ
N_LAT = DEC_BATCH * DEC_SEQ
N_TOK = N_CTX + N_LAT
UNIT = 1024
N_UNITS = N_TOK // UNIT
CTX_UNITS = N_CTX // UNIT
REQ_PER_CTX_UNIT = UNIT // SEQ
SLOTS = 128
CAP_CTX = 2 * SEQ // E
CAP_LAT = 2 * DEC_SEQ // E
N_SLOT = N_UNITS * SLOTS
N_ROW = 16
LANES = 128
MIB = 1024 * 1024


def _cparams(sem, vmem_mib):
    return pltpu.CompilerParams(dimension_semantics=sem, vmem_limit_bytes=vmem_mib * MIB)


def _resident(block_shape, index_map):
    return pl.BlockSpec(block_shape, index_map, pipeline_mode=pl.Buffered(1))


def _dot(a, b):
    return jnp.dot(a, b, preferred_element_type=F32)


def _dot_nt(a, b):
    return lax.dot_general(a, b, (((1,), (1,)), ((), ())), preferred_element_type=F32)


def _dot_tn(a, b):
    return lax.dot_general(a, b, (((0,), (0,)), ((), ())), preferred_element_type=F32)


def _silu(x):
    return x / (1.0 + jnp.exp(-x))


def _norm_mod(x, gain, shift, scale):
    y = x * lax.rsqrt(jnp.mean(x * x, axis=-1, keepdims=True) + RMS_EPS)
    return (y * gain) * (1.0 + scale) + shift


def _mod_row_of_tile(i, tile):
    per_req = DEC_SEQ // tile
    n_ctx_tiles = N_CTX // tile
    return jnp.where(i < n_ctx_tiles, 0, 1 + (i - n_ctx_tiles) // per_req)


def _mod_spec(layer, which):
    return pl.BlockSpec((None, N_ROW, D), lambda *ids: (layer, 0, which))


def _mod_vec(ref, tile, axis=0):
    return ref[pl.ds(_mod_row_of_tile(pl.program_id(axis), tile), 1), :]


def _gain_spec():
    return pl.BlockSpec((DEPTH, D), lambda *ids: (0, 0))


def _token_specs(srcs, tile, width):
    if len(srcs) == 1:
        return [pl.BlockSpec((tile, width), lambda i: (i, 0))]
    n_ctx_tiles = N_CTX // tile
    return [pl.BlockSpec((tile, width), lambda i: (jnp.minimum(i, n_ctx_tiles - 1), 0)),
            pl.BlockSpec((tile, width), lambda i: (jnp.maximum(i - n_ctx_tiles, 0), 0))]


def _read_tokens(refs, tile):
    if len(refs) == 1:
        return refs[0][...]
    return jnp.where(pl.program_id(0) < N_CTX // tile, refs[0][...], refs[1][...])


def _adaln_kernel(c_ref, w_ref, b_ref, o_ref):
    a = _silu(c_ref[...]).astype(BF16)
    o_ref[0] = _dot(a, w_ref[0].astype(BF16)) + b_ref[pl.ds(pl.program_id(0), 1), :]


def _adaln(cond, w_mod, b_mod):
    tn = 1024
    return pl.pallas_call(
        _adaln_kernel,
        grid=(DEPTH, N_MOD * D // tn),
        in_specs=[pl.BlockSpec((N_ROW, D), lambda l, j: (0, 0)),
                  pl.BlockSpec((1, D, tn), lambda l, j: (l, 0, j)),
                  pl.BlockSpec((DEPTH, tn), lambda l, j: (0, j))],
        out_specs=pl.BlockSpec((1, N_ROW, tn), lambda l, j: (l, 0, j)),
        out_shape=jax.ShapeDtypeStruct((DEPTH, N_ROW, N_MOD * D), F32),
        compiler_params=_cparams(("arbitrary", "arbitrary"), 40),
        name="adaln",
    )(cond, w_mod, b_mod)


def _cast_kernel(w_ref, o_ref):
    o_ref[...] = w_ref[...].astype(BF16)


def _to_bf16(w, layer):
    rows, cols = w.shape[1:]
    tr = 256
    return pl.pallas_call(
        _cast_kernel,
        grid=(rows // tr,),
        in_specs=[pl.BlockSpec((None, tr, cols), lambda i: (layer, i, 0))],
        out_specs=pl.BlockSpec((tr, cols), lambda i: (i, 0)),
        out_shape=jax.ShapeDtypeStruct((rows, cols), BF16),
        compiler_params=_cparams(("arbitrary",), 32),
        name="cast_bf16",
    )(w)


_INPROJ_TILE = 256


def _inproj_kernel(*refs, n_src, n_prev, layer):
    x_refs = refs[:n_src]
    g_ref, sh_ref, sc_ref, w_ref = refs[n_src:n_src + 4]
    q_ref, k_ref, v_ref, u_ref, kf_ref, vf_ref = refs[n_src + 4 + n_prev:]
    tm = _INPROJ_TILE
    h = _norm_mod(_read_tokens(x_refs, tm), g_ref[layer:layer + 1, :], _mod_vec(sh_ref, tm),
                  _mod_vec(sc_ref, tm)).astype(BF16)
    q_ref[...] = (_dot(h, w_ref[:, 0:NA_W]) * (DH ** -0.5)).astype(BF16)
    k = _dot(h, w_ref[:, NA_W:2 * NA_W])
    v = _dot(h, w_ref[:, 2 * NA_W:3 * NA_W])
    k_ref[...] = k.astype(BF16)
    v_ref[...] = v.astype(BF16)
    u_ref[...] = _dot(h, w_ref[:, 3 * NA_W:IN_W])

    @pl.when(pl.program_id(0) < N_CTX // tm)
    def _():
        kf_ref[0, 0] = k
        vf_ref[0, 0] = v


def _inproj(srcs, mods, norm_gain, w_in_bf, layer, kv_prev):
    assert SEQ == _INPROJ_TILE
    tm = _INPROJ_TILE
    n_ctx_tiles = N_CTX // tm
    tok = lambda: pl.BlockSpec((tm, NA_W), lambda i: (i, 0))
    ctx_only = lambda: pl.BlockSpec((1, 1, SEQ, NA_W), lambda i: (jnp.minimum(i, n_ctx_tiles - 1), layer, 0, 0))
    n_in = len(srcs) + 4
    return pl.pallas_call(
        functools.partial(_inproj_kernel, n_src=len(srcs), n_prev=len(kv_prev), layer=layer),
        grid=(N_TOK // tm,),
        in_specs=_token_specs(srcs, tm, D) + [
            _gain_spec(), _mod_spec(layer, 0), _mod_spec(layer, 1),
            _resident((D, IN_W), lambda i: (0, 0))] + [pl.BlockSpec(memory_space=pl.ANY)] * len(kv_prev),
        out_specs=[tok(), tok(), tok(), tok(), ctx_only(), ctx_only()],
        out_shape=[jax.ShapeDtypeStruct((N_TOK, NA_W), BF16)] * 3
        + [jax.ShapeDtypeStruct((N_TOK, POOL_W), F32)]
        + [jax.ShapeDtypeStruct((BATCH, DEPTH, SEQ, NA_W), F32)] * 2,
        input_output_aliases={n_in + j: 4 + j for j in range(len(kv_prev))},
        compiler_params=_cparams(("arbitrary",), 48),
        name="inproj",
    )(*srcs, norm_gain, mods, mods, w_in_bf, *kv_prev)


def _na_window_start(qr):
    return min(max(qr - NA_ROWS // 2, 0), GRID_H - NA_ROWS)


def _na_bias_kernel(rpb_ref, o_ref):
    qc = lax.broadcasted_iota(jnp.int32, (GRID_W, LANES), 0)
    kc = lax.broadcasted_iota(jnp.int32, (GRID_W, LANES), 1)
    cs = jnp.clip(qc - NA_COLS // 2, 0, GRID_W - NA_COLS)
    in_col = (kc >= cs) & (kc < cs + NA_COLS)
    by_row_offset = []
    for dr in range(2 * NA_ROWS - 1):
        row = jnp.broadcast_to(rpb_ref[0, dr:dr + 1, :], (GRID_W, LANES))
        toep = pltpu.roll(row, LANES - (NA_COLS - 1), 1, stride=1, stride_axis=0)
        by_row_offset.append(jnp.where(in_col, toep, NEG_INF)[:, 0:GRID_W])
    masked = jnp.full((GRID_W, GRID_W), NEG_INF, F32)
    for qr in range(GRID_H):
        rs = _na_window_start(qr)
        for kr in range(GRID_H):
            blk = by_row_offset[kr - qr + NA_ROWS - 1] if rs <= kr < rs + NA_ROWS else masked
            o_ref[0, qr * GRID_W:(qr + 1) * GRID_W, kr * GRID_W:(kr + 1) * GRID_W] = blk


def _na_bias_table(rpb_l):
    rpb_pad = jnp.pad(rpb_l, ((0, 0), (0, 0), (0, LANES - rpb_l.shape[-1])))
    n_dr = rpb_l.shape[1]
    return pl.pallas_call(
        _na_bias_kernel,
        grid=(H,),
        in_specs=[pl.BlockSpec((1, n_dr, LANES), lambda h: (h, 0, 0))],
        out_specs=pl.BlockSpec((1, DEC_SEQ, DEC_SEQ), lambda h: (h, 0, 0)),
        out_shape=jax.ShapeDtypeStruct((H, DEC_SEQ, DEC_SEQ), F32),
        compiler_params=_cparams(("arbitrary",), 24),
        name="na_bias",
    )(rpb_pad)


_NA_Q_BLOCK = 4 * GRID_W


def _na_key_range(qb):
    rows_per_block = _NA_Q_BLOCK // GRID_W
    lo = _na_window_start(qb * rows_per_block) * GRID_W
    hi = (_na_window_start((qb + 1) * rows_per_block - 1) + NA_ROWS) * GRID_W
    return (lo // _NA_Q_BLOCK * _NA_Q_BLOCK, -(-hi // _NA_Q_BLOCK) * _NA_Q_BLOCK)


_NA_KEY_RANGES = [_na_key_range(qb) for qb in range(DEC_SEQ // _NA_Q_BLOCK)]


def _attn_kernel(q_ref, k_ref, v_ref, ck_ref, cv_ref, tab_ref, o_ref, kc_ref, vv_ref):
    lane = lax.broadcasted_iota(jnp.int32, (1, 2 * DH), 1)
    head_lanes = [lane < DH, lane >= DH]
    vv_ref[0:UNIT, 0:2 * DH] = v_ref[...]
    vv_ref[:, 2 * DH:4 * DH] = jnp.ones((UNIT + PAST, 2 * DH), BF16)

    def head_queries(rows):
        q = q_ref[rows, :]
        return [jnp.where(m, q, jnp.zeros_like(q)) for m in head_lanes]

    def probs(scores):
        m = functools.reduce(jnp.maximum, [jnp.max(s, axis=-1, keepdims=True) for s in scores])
        return [jnp.exp(s - m).astype(BF16) for s in scores]

    def merge_heads(acc):
        outs = [a[:, 0:2 * DH] / a[:, 2 * DH:4 * DH] for a in acc]
        return jnp.where(head_lanes[0], outs[0], outs[1]).astype(o_ref.dtype)

    @pl.when(pl.program_id(1) < CTX_UNITS)
    def _():
        for r in range(REQ_PER_CTX_UNIT):
            rows = slice(r * SEQ, (r + 1) * SEQ)
            acc = []
            for qh in head_queries(rows):
                (p,) = probs([_dot_nt(qh, k_ref[rows, :])])
                acc.append(_dot(p, vv_ref[rows, :]))
            o_ref[rows, :] = merge_heads(acc)

    @pl.when(pl.program_id(1) >= CTX_UNITS)
    def _():
        kc_ref[...] = ck_ref[0, 0].astype(BF16)
        vv_ref[UNIT:UNIT + PAST, 0:2 * DH] = cv_ref[0, 0].astype(BF16)
        for qb, (k_lo, k_hi) in enumerate(_NA_KEY_RANGES):
            rows = slice(qb * _NA_Q_BLOCK, (qb + 1) * _NA_Q_BLOCK)
            keys = slice(k_lo, k_hi)
            acc = []
            for hh, qh in enumerate(head_queries(rows)):
                s_lat = _dot_nt(qh, k_ref[keys, :]) + tab_ref[hh, rows, keys]
                s_ctx = _dot_nt(qh, kc_ref[...])
                p_lat, p_ctx = probs([s_lat, s_ctx])
                acc.append(_dot(p_lat, vv_ref[keys, :]) + _dot(p_ctx, vv_ref[UNIT:UNIT + PAST, :]))
            o_ref[rows, :] = merge_heads(acc)


def _attention(q, k, v, cache_k, cache_v, tab, layer):
    blk = lambda: pl.BlockSpec((UNIT, 2 * DH), lambda g, u: (u, g))
    cache = lambda: pl.BlockSpec((1, 1, PAST, 2 * DH), lambda g, u: (jnp.maximum(u - CTX_UNITS, 0), layer, 0, g))
    return pl.pallas_call(
        _attn_kernel,
        grid=(H // 2, N_UNITS),
        in_specs=[blk(), blk(), blk(), cache(), cache(),
                  pl.BlockSpec((2, DEC_SEQ, DEC_SEQ), lambda g, u: (g, 0, 0))],
        out_specs=blk(),
        out_shape=jax.ShapeDtypeStruct((N_TOK, NA_W), BF16),
        scratch_shapes=[pltpu.VMEM((PAST, 2 * DH), BF16),
                        pltpu.VMEM((UNIT + PAST, 4 * DH), BF16)],
        compiler_params=_cparams(("arbitrary", "arbitrary"), 44),
        name="attention",
    )(q, k, v, cache_k, cache_v, tab)


def _pool_kernel(u_ref, wp_ref, ps_ref, o_ref, *, layer):
    n = jnp.where(pl.program_id(0) < CTX_UNITS, SEQ, DEC_SEQ)
    t = lax.broadcasted_iota(jnp.int32, (UNIT, PG), 0)
    p = jnp.bitwise_and(t, n - 1)

    def fwd(a, s):
        return jnp.where(p < n - s, pltpu.roll(a, UNIT - s, 0), 0.0)

    def bwd(a, s):
        return jnp.where(p >= s, pltpu.roll(a, s, 0), 0.0)

    for gi, w in enumerate(POOL_WINDOWS):
        half = w // 2
        cols = slice(gi * PG, (gi + 1) * PG)
        g = u_ref[:, cols]
        f, b, s = g, g, 1
        while s < half:
            f = f + fwd(f, s)
            b = b + bwd(b, s)
            s *= 2
        tot = f + bwd(b, 1)
        cnt = (jnp.minimum(p + half, n) - jnp.maximum(p - half, 0)).astype(F32)
        dlt = tot / cnt - g
        out = _dot(dlt.astype(BF16), wp_ref[gi].astype(BF16)) * ps_ref[layer:layer + 1, cols]
        o_ref[:, cols] = out.astype(o_ref.dtype)


def _pool_mixer(u, w_pool, pool_scale, layer):
    return pl.pallas_call(
        functools.partial(_pool_kernel, layer=layer),
        grid=(N_UNITS,),
        in_specs=[pl.BlockSpec((UNIT, POOL_W), lambda i: (i, 0)),
                  pl.BlockSpec((None, len(POOL_WINDOWS), PG, PG), lambda i: (layer, 0, 0, 0)),
                  pl.BlockSpec((DEPTH, POOL_W), lambda i: (0, 0))],
        out_specs=pl.BlockSpec((UNIT, POOL_W), lambda i: (i, 0)),
        out_shape=jax.ShapeDtypeStruct((N_TOK, POOL_W), BF16),
        compiler_params=_cparams(("arbitrary",), 40),
        name="pool",
    )(u, w_pool, pool_scale)


_OUTPROJ_TILE = 512


def _outproj_kernel(*refs, n_src, layer):
    x_refs = refs[:n_src]
    att_ref, pool_ref, w_ref, gate_ref, g2_ref, sh_ref, sc_ref, wr_ref, y_ref, h_ref, lg_ref = refs[n_src:]
    tm = _OUTPROJ_TILE
    mix = _dot(att_ref[...], w_ref[0:NA_W, :]) + _dot(pool_ref[...], w_ref[NA_W:NA_W + POOL_W, :])
    y = _read_tokens(x_refs, tm) + _mod_vec(gate_ref, tm) * mix
    y_ref[...] = y
    h = _norm_mod(y, g2_ref[layer:layer + 1, :], _mod_vec(sh_ref, tm), _mod_vec(sc_ref, tm)).astype(BF16)
    h_ref[...] = h
    lg_ref[...] = _dot(h, wr_ref[...])


def _outproj(srcs, att, pool, mods, norm_gain, w_out_bf, w_router_pad, layer):
    tm = _OUTPROJ_TILE
    row = lambda width: pl.BlockSpec((tm, width), lambda i: (i, 0))
    return pl.pallas_call(
        functools.partial(_outproj_kernel, n_src=len(srcs), layer=layer),
        grid=(N_TOK // tm,),
        in_specs=_token_specs(srcs, tm, D) + [
            row(NA_W), row(POOL_W),
            _resident((D, D), lambda i: (0, 0)),
            _mod_spec(layer, 2), _gain_spec(), _mod_spec(layer, 3), _mod_spec(layer, 4),
            _resident((D, LANES), lambda i: (0, 0))],
        out_specs=[row(D), row(D), row(LANES)],
        out_shape=[jax.ShapeDtypeStruct((N_TOK, D), F32),
                   jax.ShapeDtypeStruct((N_TOK, D), BF16),
                   jax.ShapeDtypeStruct((N_TOK, LANES), F32)],
        compiler_params=_cparams(("arbitrary",), 52),
        name="outproj",
    )(*srcs, att, pool, w_out_bf, mods, norm_gain, mods, mods, w_router_pad)


def _choose_slots(aff, n_req, n_tok, cap):
    rows = jnp.concatenate([aff[:, r * n_tok:(r + 1) * n_tok] for r in range(n_req)], axis=0)
    thr = jnp.zeros((n_req * E, 1), jnp.int32)
    for b in range(30, -1, -1):
        cand = thr | (1 << b)
        cnt = jnp.sum((rows >= lax.bitcast_convert_type(cand, F32)).astype(F32), axis=-1, keepdims=True)
        thr = jnp.where(cnt >= cap, cand, thr)
    thr_val = lax.bitcast_convert_type(thr, F32)
    gt = rows > thr_val
    eq = rows == thr_val
    n_gt = jnp.sum(gt.astype(F32), axis=-1, keepdims=True)
    before = (lax.broadcasted_iota(jnp.int32, (n_tok, n_tok), 0)
              < lax.broadcasted_iota(jnp.int32, (n_tok, n_tok), 1)).astype(BF16)
    eq_rank = _dot(eq.astype(BF16), before)
    sel = gt | (eq & (eq_rank < cap - n_gt))
    slot = _dot(sel.astype(BF16), before)
    pieces = [jnp.where(sel[r * E:(r + 1) * E], slot[r * E:(r + 1) * E] + r * cap, -1.0) for r in range(n_req)]
    return jnp.concatenate(pieces, axis=1) if n_req > 1 else pieces[0]


def _route_kernel(lg_ref, pos_ref, aff_ref):
    lg = lg_ref[...].T[0:E, :]
    ex = jnp.exp(lg - jnp.max(lg, axis=0, keepdims=True))
    aff = ex / jnp.sum(ex, axis=0, keepdims=True)
    aff_ref[0] = aff

    @pl.when(pl.program_id(0) < CTX_UNITS)
    def _():
        pos_ref[0] = _choose_slots(aff, REQ_PER_CTX_UNIT, SEQ, CAP_CTX)

    @pl.when(pl.program_id(0) >= CTX_UNITS)
    def _():
        pos_ref[0] = _choose_slots(aff, 1, DEC_SEQ, CAP_LAT)


def _route(logits):
    out_blk = lambda: pl.BlockSpec((1, E, UNIT), lambda u: (u, 0, 0))
    return pl.pallas_call(
        _route_kernel,
        grid=(N_UNITS,),
        in_specs=[pl.BlockSpec((UNIT, LANES), lambda u: (u, 0))],
        out_specs=[out_blk(), out_blk()],
        out_shape=[jax.ShapeDtypeStruct((N_UNITS, E, UNIT), F32)] * 2,
        compiler_params=_cparams(("arbitrary",), 32),
        name="route",
    )(logits)


_COL_CHUNK = 512


def _onehot_rows(pos, p_ref, n_slots, first_slot):
    width = pos.shape[1]
    slot_id = (lax.broadcasted_iota(jnp.int32, (n_slots, width), 0) + first_slot).astype(F32)
    matches = []
    for e in range(E):
        match = pos[e:e + 1, :] == slot_id
        p_ref[e * n_slots:(e + 1) * n_slots, 0:width] = jnp.where(match, 1.0, 0.0).astype(BF16)
        matches.append(match)
    return matches


def _gather_kernel(pos_ref, aff_ref, h_ref, xs_ref, gs_ref, p_ref):
    def gather(tokens, slots, first_slot):
        n = slots.stop - slots.start
        aff = aff_ref[0, :, tokens]
        matches = _onehot_rows(pos_ref[0, :, tokens], p_ref, n, first_slot)
        for e in range(E):
            gate = jnp.sum(jnp.where(matches[e], aff[e:e + 1, :], 0.0), axis=-1, keepdims=True)
            gs_ref[e, slots, :] = jnp.broadcast_to(gate, (n, LANES))
        width = tokens.stop - tokens.start
        for c in range(D // _COL_CHUNK):
            cols = slice(c * _COL_CHUNK, (c + 1) * _COL_CHUNK)
            r = _dot(p_ref[0:E * n, 0:width], h_ref[tokens, cols])
            xs_ref[:, slots, cols] = r.reshape(E, n, _COL_CHUNK).astype(BF16)

    @pl.when(pl.program_id(0) < CTX_UNITS)
    def _():
        for r in range(REQ_PER_CTX_UNIT):
            gather(slice(r * SEQ, (r + 1) * SEQ), slice(r * CAP_CTX, (r + 1) * CAP_CTX), r * CAP_CTX)

    @pl.when(pl.program_id(0) >= CTX_UNITS)
    def _():
        gather(slice(0, UNIT), slice(0, SLOTS), 0)


def _gather(pos, aff, h):
    unit3 = lambda: pl.BlockSpec((1, E, UNIT), lambda u: (u, 0, 0))
    return pl.pallas_call(
        _gather_kernel,
        grid=(N_UNITS,),
        in_specs=[unit3(), unit3(), pl.BlockSpec((UNIT, D), lambda u: (u, 0))],
        out_specs=[pl.BlockSpec((E, SLOTS, D), lambda u: (0, u, 0)),
                   pl.BlockSpec((E, SLOTS, LANES), lambda u: (0, u, 0))],
        out_shape=[jax.ShapeDtypeStruct((E, N_SLOT, D), BF16),
                   jax.ShapeDtypeStruct((E, N_SLOT, LANES), F32)],
        scratch_shapes=[pltpu.VMEM((E * SLOTS, UNIT), BF16)],
        compiler_params=_cparams(("arbitrary",), 48),
        name="gather",
    )(pos, aff, h)


_FFN_ROWS = 256
_FFN_COLS = 256


def _ffn_up_kernel(xs_ref, wg_ref, wu_ref, o_ref):
    for c in range(wg_ref.shape[1] // _FFN_COLS):
        cols = slice(c * _FFN_COLS, (c + 1) * _FFN_COLS)
        wg = wg_ref[:, cols].astype(BF16)
        wu = wu_ref[:, cols].astype(BF16)
        for m in range(N_SLOT // _FFN_ROWS):
            rows = slice(m * _FFN_ROWS, (m + 1) * _FFN_ROWS)
            x = xs_ref[rows, :]
            o_ref[rows, cols] = (_silu(_dot(x, wg)) * _dot(x, wu)).astype(o_ref.dtype)


def _ffn_up(xs, w_gate, w_up, layer):
    tf = 512
    wspec = lambda: pl.BlockSpec((None, None, D, tf), lambda e, j: (layer, e, 0, j))
    return pl.pallas_call(
        _ffn_up_kernel,
        grid=(E, D_EXP // tf),
        in_specs=[pl.BlockSpec((None, N_SLOT, D), lambda e, j: (e, 0, 0)), wspec(), wspec()],
        out_specs=pl.BlockSpec((None, N_SLOT, tf), lambda e, j: (e, 0, j)),
        out_shape=jax.ShapeDtypeStruct((E, N_SLOT, D_EXP), BF16),
        compiler_params=_cparams(("arbitrary", "arbitrary"), 48),
        name="ffn_up",
    )(xs, w_gate, w_up)


def _ffn_down_kernel(h_ref, wd_ref, gs_ref, o_ref):
    for c in range(wd_ref.shape[1] // _FFN_COLS):
        cols = slice(c * _FFN_COLS, (c + 1) * _FFN_COLS)
        wd = wd_ref[:, cols].astype(BF16)
        for m in range(N_SLOT // _FFN_ROWS):
            rows = slice(m * _FFN_ROWS, (m + 1) * _FFN_ROWS)
            gate = jnp.concatenate([gs_ref[rows, :]] * (_FFN_COLS // LANES), axis=1)
            o_ref[rows, cols] = (_dot(h_ref[rows, :], wd) * gate).astype(o_ref.dtype)


def _ffn_down(hcur, w_down, gslot, layer):
    tn = _COL_CHUNK
    return pl.pallas_call(
        _ffn_down_kernel,
        grid=(E, D // tn),
        in_specs=[pl.BlockSpec((None, N_SLOT, D_EXP), lambda e, j: (e, 0, 0)),
                  pl.BlockSpec((None, None, D_EXP, tn), lambda e, j: (layer, e, 0, j)),
                  pl.BlockSpec((None, N_SLOT, LANES), lambda e, j: (e, 0, 0))],
        out_specs=pl.BlockSpec((None, None, N_SLOT, tn), lambda e, j: (e, j, 0, 0)),
        out_shape=jax.ShapeDtypeStruct((E, D // tn, N_SLOT, tn), BF16),
        compiler_params=_cparams(("arbitrary", "arbitrary"), 32),
        name="ffn_down",
    )(hcur, w_down, gslot)


_SC_TOK = 256


def _scatter_kernel(pos_ref, ys_ref, y_ref, gate_ref, *rest, final):
    if final:
        gain_ref, oc_ref, ol_ref, p_ref = rest
    else:
        o_ref, p_ref = rest

    def emit(updates, is_ctx):
        y = y_ref[...] + _mod_vec(gate_ref, UNIT) * jnp.concatenate(updates, axis=1)
        if not final:
            o_ref[...] = y
        else:
            r = y * lax.rsqrt(jnp.mean(y * y, axis=-1, keepdims=True) + RMS_EPS) * gain_ref[...]
            (oc_ref if is_ctx else ol_ref)[...] = r

    @pl.when(pl.program_id(0) < CTX_UNITS)
    def _():
        first = pl.multiple_of(pl.program_id(1) * CAP_CTX, CAP_CTX)
        _onehot_rows(pos_ref[0], p_ref, CAP_CTX, first)
        emit([_dot_tn(p_ref[0:E * CAP_CTX, :],
                      ys_ref[:, c, pl.ds(first, CAP_CTX), :].reshape(E * CAP_CTX, _COL_CHUNK))
              for c in range(D // _COL_CHUNK)], True)

    @pl.when(pl.program_id(0) >= CTX_UNITS)
    def _():
        _onehot_rows(pos_ref[0], p_ref, SLOTS, 0)
        emit([_dot_tn(p_ref[...], ys_ref[:, c, :, :].reshape(E * SLOTS, _COL_CHUNK))
              for c in range(D // _COL_CHUNK)], False)


def _scatter(pos, ys, y, mods, layer, final_gain=None):
    assert _SC_TOK == SEQ
    per_unit = UNIT // _SC_TOK
    n_ctx_tiles = N_CTX // _SC_TOK
    final = final_gain is not None
    tok = lambda: pl.BlockSpec((_SC_TOK, D), lambda u, s: (u * per_unit + s, 0))
    in_specs = [pl.BlockSpec((1, E, _SC_TOK), lambda u, s: (u, 0, s)),
                pl.BlockSpec((E, D // _COL_CHUNK, SLOTS, _COL_CHUNK), lambda u, s: (0, 0, u, 0)),
                tok(),
                _mod_spec(layer, 5)]
    args = [pos, ys, y, mods]
    if final:
        in_specs.append(pl.BlockSpec((1, D), lambda u, s: (0, 0)))
        args.append(final_gain.reshape(1, D))
        out_specs = [pl.BlockSpec((_SC_TOK, D), lambda u, s: (jnp.minimum(u * per_unit + s, n_ctx_tiles - 1), 0)),
                     pl.BlockSpec((_SC_TOK, D), lambda u, s: (jnp.maximum(u * per_unit + s - n_ctx_tiles, 0), 0))]
        out_shape = [jax.ShapeDtypeStruct((N_CTX, D), F32), jax.ShapeDtypeStruct((N_LAT, D), F32)]
    else:
        out_specs = tok()
        out_shape = jax.ShapeDtypeStruct((N_TOK, D), F32)
    return pl.pallas_call(
        functools.partial(_scatter_kernel, final=final),
        grid=(N_UNITS, per_unit),
        in_specs=in_specs,
        out_specs=out_specs,
        out_shape=out_shape,
        scratch_shapes=[pltpu.VMEM((E * SLOTS, _SC_TOK), BF16)],
        compiler_params=_cparams(("arbitrary", "arbitrary"), 40),
        name="scatter",
    )(*args)


def kernel(x_prompt, x_sample, cache_k, cache_v, c, c_ctx, w_mod, b_mod, norm_mix, norm_ffn, w_in, rpb,
           w_pool, pool_scale, w_out, w_router, w_gate, w_up, w_down, norm_final):
    srcs = [x_prompt.reshape(N_CTX, D), x_sample.reshape(N_LAT, D)]
    cond = jnp.zeros((N_ROW, D), F32).at[0].set(c_ctx).at[1:1 + DEC_BATCH].set(c)
    mods = _adaln(cond, w_mod, b_mod)
    ck = cache_k.reshape(DEC_BATCH, DEPTH, PAST, NA_W)
    cv = cache_v.reshape(DEC_BATCH, DEPTH, PAST, NA_W)

    new_kv = []
    for l in range(DEPTH):
        q, k, v, u, *new_kv = _inproj(srcs, mods, norm_mix, _to_bf16(w_in, l), l, new_kv)
        att = _attention(q, k, v, ck, cv, _na_bias_table(rpb[l]), l)
        pool = _pool_mixer(u, w_pool, pool_scale, l)

        w_router_pad = jnp.pad(w_router[l], ((0, 0), (0, LANES - E))).astype(BF16)
        y, h, logits = _outproj(srcs, att, pool, mods, norm_ffn, _to_bf16(w_out, l), w_router_pad, l)

        pos, aff = _route(logits)
        xs, gslot = _gather(pos, aff, h)
        ys = _ffn_down(_ffn_up(xs, w_gate, w_up, l), w_down, gslot, l)
        srcs = _scatter(pos, ys, y, mods, l, norm_final if l == DEPTH - 1 else None)
        srcs = list(srcs) if l == DEPTH - 1 else [srcs]

    y_prompt, y_sample = srcs
    new_k, new_v = new_kv
    return (y_prompt.reshape(BATCH, SEQ, D), y_sample.reshape(DEC_BATCH, DEC_SEQ, D),
            new_k.reshape(BATCH, DEPTH, SEQ, H, DH), new_v.reshape(BATCH, DEPTH, SEQ, H, DH))
```

```python
import functools

import jax
import jax.numpy as jnp
from jax import lax
from jax.experimental import pallas as pl
from jax.experimental.pallas import tpu as pltpu

F32 = jnp.float32
BF16 = jnp.bfloat16

D = 2048
BATCH, SEQ = 16, 256
DEPTH = 2
DEC_BATCH, DEC_SEQ = 8, 1024
PAST = 512
GRID_W = 64
GRID_H = DEC_SEQ // GRID_W
H, DH = 16, 64
NA_W = H * DH
POOL_WINDOWS = (2, 4, 8, 16)
PG = 256
POOL_W = 1024
IN_W = 3 * NA_W + POOL_W
NA_ROWS, NA_COLS = 8, 16
E = 16
D_EXP = 1024
N_MOD = 6
RMS_EPS = 1e-6
NEG_INF = -1e30

N_CTX = BATCH * SEQ
N_LAT = DEC_BATCH * DEC_SEQ
N_TOK = N_CTX + N_LAT
UNIT = 1024
N_UNITS = N_TOK // UNIT
CTX_UNITS = N_CTX // UNIT
REQ_PER_CTX_UNIT = UNIT // SEQ
SLOTS = 128
CAP_CTX = 2 * SEQ // E
CAP_LAT = 2 * DEC_SEQ // E
N_SLOT = N_UNITS * SLOTS
N_ROW = 16
LANES = 128
MIB = 1024 * 1024


def _cparams(sem, vmem_mib):
    return pltpu.CompilerParams(dimension_semantics=sem, vmem_limit_bytes=vmem_mib * MIB)


def _resident(block_shape, index_map):
    return pl.BlockSpec(block_shape, index_map, pipeline_mode=pl.Buffered(1))


def _dot(a, b):
    return jnp.dot(a, b, preferred_element_type=F32)


def _dot_nt(a, b):
    return lax.dot_general(a, b, (((1,), (1,)), ((), ())), preferred_element_type=F32)


def _dot_tn(a, b):
    return lax.dot_general(a, b, (((0,), (0,)), ((), ())), preferred_element_type=F32)


def _silu(x):
    return x / (1.0 + jnp.exp(-x))


def _norm_mod(x, gain, shift, scale):
    y = x * lax.rsqrt(jnp.mean(x * x, axis=-1, keepdims=True) + RMS_EPS)
    return (y * gain) * (1.0 + scale) + shift


def _mod_row_of_tile(i, tile):
    per_req = DEC_SEQ // tile
    n_ctx_tiles = N_CTX // tile
    return jnp.where(i < n_ctx_tiles, 0, 1 + (i - n_ctx_tiles) // per_req)


def _mod_spec(layer, which):
    return pl.BlockSpec((None, N_ROW, D), lambda *ids: (layer, 0, which))


def _mod_vec(ref, tile, axis=0):
    return ref[pl.ds(_mod_row_of_tile(pl.program_id(axis), tile), 1), :]


def _gain_spec():
    return pl.BlockSpec((DEPTH, D), lambda *ids: (0, 0))


def _token_specs(srcs, tile, width):
    if len(srcs) == 1:
        return [pl.BlockSpec((tile, width), lambda i: (i, 0))]
    n_ctx_tiles = N_CTX // tile
    return [pl.BlockSpec((tile, width), lambda i: (jnp.minimum(i, n_ctx_tiles - 1), 0)),
            pl.BlockSpec((tile, width), lambda i: (jnp.maximum(i - n_ctx_tiles, 0), 0))]


def _read_tokens(refs, tile):
    if len(refs) == 1:
        return refs[0][...]
    return jnp.where(pl.program_id(0) < N_CTX // tile, refs[0][...], refs[1][...])


def _adaln_kernel(c_ref, w_ref, b_ref, o_ref):
    a = _silu(c_ref[...]).astype(BF16)
    o_ref[0] = _dot(a, w_ref[0].astype(BF16)) + b_ref[pl.ds(pl.program_id(0), 1), :]


def _adaln(cond, w_mod, b_mod):
    tn = 1024
    return pl.pallas_call(
        _adaln_kernel,
        grid=(DEPTH, N_MOD * D // tn),
        in_specs=[pl.BlockSpec((N_ROW, D), lambda l, j: (0, 0)),
                  pl.BlockSpec((1, D, tn), lambda l, j: (l, 0, j)),
                  pl.BlockSpec((DEPTH, tn), lambda l, j: (0, j))],
        out_specs=pl.BlockSpec((1, N_ROW, tn), lambda l, j: (l, 0, j)),
        out_shape=jax.ShapeDtypeStruct((DEPTH, N_ROW, N_MOD * D), F32),
        compiler_params=_cparams(("arbitrary", "arbitrary"), 40),
        name="adaln",
    )(cond, w_mod, b_mod)


def _cast_kernel(w_ref, o_ref):
    o_ref[...] = w_ref[...].astype(BF16)


def _to_bf16(w, layer):
    rows, cols = w.shape[1:]
    tr = 256
    return pl.pallas_call(
        _cast_kernel,
        grid=(rows // tr,),
        in_specs=[pl.BlockSpec((None, tr, cols), lambda i: (layer, i, 0))],
        out_specs=pl.BlockSpec((tr, cols), lambda i: (i, 0)),
        out_shape=jax.ShapeDtypeStruct((rows, cols), BF16),
        compiler_params=_cparams(("arbitrary",), 32),
        name="cast_bf16",
    )(w)


_INPROJ_TILE = 256


def _inproj_kernel(*refs, n_src, n_prev, layer):
    x_refs = refs[:n_src]
    g_ref, sh_ref, sc_ref, w_ref = refs[n_src:n_src + 4]
    q_ref, k_ref, v_ref, u_ref, kf_ref, vf_ref = refs[n_src + 4 + n_prev:]
    tm = _INPROJ_TILE
    h = _norm_mod(_read_tokens(x_refs, tm), g_ref[layer:layer + 1, :], _mod_vec(sh_ref, tm),
                  _mod_vec(sc_ref, tm)).astype(BF16)
    q_ref[...] = (_dot(h, w_ref[:, 0:NA_W]) * (DH ** -0.5)).astype(BF16)
    k = _dot(h, w_ref[:, NA_W:2 * NA_W])
    v = _dot(h, w_ref[:, 2 * NA_W:3 * NA_W])
    k_ref[...] = k.astype(BF16)
    v_ref[...] = v.astype(BF16)
    u_ref[...] = _dot(h, w_ref[:, 3 * NA_W:IN_W])

    @pl.when(pl.program_id(0) < N_CTX // tm)
    def _():
        kf_ref[0, 0] = k
        vf_ref[0, 0] = v


def _inproj(srcs, mods, norm_gain, w_in_bf, layer, kv_prev):
    assert SEQ == _INPROJ_TILE
    tm = _INPROJ_TILE
    n_ctx_tiles = N_CTX // tm
    tok = lambda: pl.BlockSpec((tm, NA_W), lambda i: (i, 0))
    ctx_only = lambda: pl.BlockSpec((1, 1, SEQ, NA_W), lambda i: (jnp.minimum(i, n_ctx_tiles - 1), layer, 0, 0))
    n_in = len(srcs) + 4
    return pl.pallas_call(
        functools.partial(_inproj_kernel, n_src=len(srcs), n_prev=len(kv_prev), layer=layer),
        grid=(N_TOK // tm,),
        in_specs=_token_specs(srcs, tm, D) + [
            _gain_spec(), _mod_spec(layer, 0), _mod_spec(layer, 1),
            _resident((D, IN_W), lambda i: (0, 0))] + [pl.BlockSpec(memory_space=pl.ANY)] * len(kv_prev),
        out_specs=[tok(), tok(), tok(), tok(), ctx_only(), ctx_only()],
        out_shape=[jax.ShapeDtypeStruct((N_TOK, NA_W), BF16)] * 3
        + [jax.ShapeDtypeStruct((N_TOK, POOL_W), F32)]
        + [jax.ShapeDtypeStruct((BATCH, DEPTH, SEQ, NA_W), F32)] * 2,
        input_output_aliases={n_in + j: 4 + j for j in range(len(kv_prev))},
        compiler_params=_cparams(("arbitrary",), 48),
        name="inproj",
    )(*srcs, norm_gain, mods, mods, w_in_bf, *kv_prev)


def _na_window_start(qr):
    return min(max(qr - NA_ROWS // 2, 0), GRID_H - NA_ROWS)


def _na_bias_kernel(rpb_ref, o_ref):
    assert LANES == 2 * GRID_W
    qc = lax.broadcasted_iota(jnp.int32, (GRID_W, LANES), 0)
    lane = lax.broadcasted_iota(jnp.int32, (GRID_W, LANES), 1)
    kc = jnp.bitwise_and(lane, GRID_W - 1)
    cs = jnp.clip(qc - NA_COLS // 2, 0, GRID_W - NA_COLS)
    in_col = (kc >= cs) & (kc < cs + NA_COLS)
    keep_lanes = {(True, True): in_col, (True, False): in_col & (lane < GRID_W),
                  (False, True): in_col & (lane >= GRID_W)}
    n_dr = 2 * NA_ROWS - 1
    zero_rows = jnp.zeros((GRID_W, LANES), F32)

    def rpb_rows(dr):
        if not 0 <= dr < n_dr:
            return zero_rows
        return jnp.broadcast_to(rpb_ref[0, dr:dr + 1, :], (GRID_W, LANES))

    pair_cache = {}

    def pair_tile(dr, use_first, use_second):
        if not (use_first or use_second):
            return jnp.full((GRID_W, LANES), NEG_INF, F32)
        key = (dr, use_first, use_second)
        if key not in pair_cache:
            src = rpb_rows(dr) + pltpu.roll(rpb_rows(dr + 1), GRID_W, 1)
            toep = pltpu.roll(src, LANES - (NA_COLS - 1), 1, stride=1, stride_axis=0)
            pair_cache[key] = jnp.where(keep_lanes[(use_first, use_second)], toep, NEG_INF)
        return pair_cache[key]

    for qr in range(GRID_H):
        rs = _na_window_start(qr)
        for kr in range(0, GRID_H, 2):
            use = [rs <= k < rs + NA_ROWS for k in (kr, kr + 1)]
            o_ref[0, qr * GRID_W:(qr + 1) * GRID_W, kr * GRID_W:(kr + 2) * GRID_W] = pair_tile(
                kr - qr + NA_ROWS - 1, use[0], use[1])


def _na_bias_table(rpb_l):
    rpb_pad = jnp.pad(rpb_l, ((0, 0), (0, 0), (0, LANES - rpb_l.shape[-1])))
    n_dr = rpb_l.shape[1]
    return pl.pallas_call(
        _na_bias_kernel,
        grid=(H,),
        in_specs=[pl.BlockSpec((1, n_dr, LANES), lambda h: (h, 0, 0))],
        out_specs=pl.BlockSpec((1, DEC_SEQ, DEC_SEQ), lambda h: (h, 0, 0)),
        out_shape=jax.ShapeDtypeStruct((H, DEC_SEQ, DEC_SEQ), F32),
        compiler_params=_cparams(("arbitrary",), 24),
        name="na_bias",
    )(rpb_pad)


_NA_Q_BLOCK = 4 * GRID_W


def _na_key_range(qb):
    rows_per_block = _NA_Q_BLOCK // GRID_W
    lo = _na_window_start(qb * rows_per_block) * GRID_W
    hi = (_na_window_start((qb + 1) * rows_per_block - 1) + NA_ROWS) * GRID_W
    return (lo // _NA_Q_BLOCK * _NA_Q_BLOCK, -(-hi // _NA_Q_BLOCK) * _NA_Q_BLOCK)


_NA_KEY_RANGES = [_na_key_range(qb) for qb in range(DEC_SEQ // _NA_Q_BLOCK)]


def _attn_kernel(q_ref, k_ref, v_ref, ck_ref, cv_ref, tab_ref, o_ref, kc_ref, vv_ref):
    lane = lax.broadcasted_iota(jnp.int32, (1, 2 * DH), 1)
    head_lanes = [lane < DH, lane >= DH]
    vv_ref[0:UNIT, 0:2 * DH] = v_ref[...]
    vv_ref[:, 2 * DH:4 * DH] = jnp.ones((UNIT + PAST, 2 * DH), BF16)

    def head_queries(rows):
        q = q_ref[rows, :]
        return [jnp.where(m, q, jnp.zeros_like(q)) for m in head_lanes]

    def probs(scores):
        m = functools.reduce(jnp.maximum, [jnp.max(s, axis=-1, keepdims=True) for s in scores])
        return [jnp.exp(s - m).astype(BF16) for s in scores]

    def merge_heads(acc):
        outs = [a[:, 0:2 * DH] / a[:, 2 * DH:4 * DH] for a in acc]
        return jnp.where(head_lanes[0], outs[0], outs[1]).astype(o_ref.dtype)

    @pl.when(pl.program_id(1) < CTX_UNITS)
    def _():
        for r in range(REQ_PER_CTX_UNIT):
            rows = slice(r * SEQ, (r + 1) * SEQ)
            acc = []
            for qh in head_queries(rows):
                (p,) = probs([_dot_nt(qh, k_ref[rows, :])])
                acc.append(_dot(p, vv_ref[rows, :]))
            o_ref[rows, :] = merge_heads(acc)

    @pl.when(pl.program_id(1) >= CTX_UNITS)
    def _():
        first_head = 2 * pl.program_id(0)

        def cached_pair(ref):
            return jnp.concatenate([ref[0, 0, pl.ds(first_head + hh, PAST, stride=H), :] for hh in range(2)],
                                   axis=1).astype(BF16)

        kc_ref[...] = cached_pair(ck_ref)
        vv_ref[UNIT:UNIT + PAST, 0:2 * DH] = cached_pair(cv_ref)
        for qb, (k_lo, k_hi) in enumerate(_NA_KEY_RANGES):
            rows = slice(qb * _NA_Q_BLOCK, (qb + 1) * _NA_Q_BLOCK)
            keys = slice(k_lo, k_hi)
            acc = []
            for hh, qh in enumerate(head_queries(rows)):
                s_lat = _dot_nt(qh, k_ref[keys, :]) + tab_ref[hh, rows, keys]
                s_ctx = _dot_nt(qh, kc_ref[...])
                p_lat, p_ctx = probs([s_lat, s_ctx])
                acc.append(_dot(p_lat, vv_ref[keys, :]) + _dot(p_ctx, vv_ref[UNIT:UNIT + PAST, :]))
            o_ref[rows, :] = merge_heads(acc)


def _attention(q, k, v, cache_k, cache_v, tab, layer):
    blk = lambda: pl.BlockSpec((UNIT, 2 * DH), lambda g, u: (u, g))
    cache = lambda: pl.BlockSpec((1, 1, PAST * H, DH), lambda g, u: (jnp.maximum(u - CTX_UNITS, 0), layer, 0, 0))
    return pl.pallas_call(
        _attn_kernel,
        grid=(H // 2, N_UNITS),
        in_specs=[blk(), blk(), blk(), cache(), cache(),
                  pl.BlockSpec((2, DEC_SEQ, DEC_SEQ), lambda g, u: (g, 0, 0))],
        out_specs=blk(),
        out_shape=jax.ShapeDtypeStruct((N_TOK, NA_W), BF16),
        scratch_shapes=[pltpu.VMEM((PAST, 2 * DH), BF16),
                        pltpu.VMEM((UNIT + PAST, 4 * DH), BF16)],
        compiler_params=_cparams(("arbitrary", "arbitrary"), 44),
        name="attention",
    )(q, k, v, cache_k, cache_v, tab)


def _pool_kernel(u_ref, wp_ref, ps_ref, o_ref, *, layer):
    n = jnp.where(pl.program_id(0) < CTX_UNITS, SEQ, DEC_SEQ)
    t = lax.broadcasted_iota(jnp.int32, (UNIT, PG), 0)
    p = jnp.bitwise_and(t, n - 1)

    def fwd(a, s):
        return jnp.where(p < n - s, pltpu.roll(a, UNIT - s, 0), 0.0)

    def bwd(a, s):
        return jnp.where(p >= s, pltpu.roll(a, s, 0), 0.0)

    for gi, w in enumerate(POOL_WINDOWS):
        half = w // 2
        cols = slice(gi * PG, (gi + 1) * PG)
        g = u_ref[:, cols]
        f, b, s = g, g, 1
        while s < half:
            f = f + fwd(f, s)
            b = b + bwd(b, s)
            s *= 2
        tot = f + bwd(b, 1)
        cnt = (jnp.minimum(p + half, n) - jnp.maximum(p - half, 0)).astype(F32)
        dlt = tot / cnt - g
        out = _dot(dlt.astype(BF16), wp_ref[gi].astype(BF16)) * ps_ref[layer:layer + 1, cols]
        o_ref[:, cols] = out.astype(o_ref.dtype)


def _pool_mixer(u, w_pool, pool_scale, layer):
    return pl.pallas_call(
        functools.partial(_pool_kernel, layer=layer),
        grid=(N_UNITS,),
        in_specs=[pl.BlockSpec((UNIT, POOL_W), lambda i: (i, 0)),
                  pl.BlockSpec((None, len(POOL_WINDOWS), PG, PG), lambda i: (layer, 0, 0, 0)),
                  pl.BlockSpec((DEPTH, POOL_W), lambda i: (0, 0))],
        out_specs=pl.BlockSpec((UNIT, POOL_W), lambda i: (i, 0)),
        out_shape=jax.ShapeDtypeStruct((N_TOK, POOL_W), BF16),
        compiler_params=_cparams(("arbitrary",), 40),
        name="pool",
    )(u, w_pool, pool_scale)


_OUTPROJ_TILE = 512


def _outproj_kernel(*refs, n_src, layer):
    x_refs = refs[:n_src]
    att_ref, pool_ref, w_ref, gate_ref, g2_ref, sh_ref, sc_ref, wr_ref, y_ref, h_ref, lg_ref = refs[n_src:]
    tm = _OUTPROJ_TILE
    mix = _dot(att_ref[...], w_ref[0:NA_W, :]) + _dot(pool_ref[...], w_ref[NA_W:NA_W + POOL_W, :])
    y = _read_tokens(x_refs, tm) + _mod_vec(gate_ref, tm) * mix
    y_ref[...] = y
    h = _norm_mod(y, g2_ref[layer:layer + 1, :], _mod_vec(sh_ref, tm), _mod_vec(sc_ref, tm)).astype(BF16)
    h_ref[...] = h
    lg_ref[...] = _dot(h, wr_ref[...])


def _outproj(srcs, att, pool, mods, norm_gain, w_out_bf, w_router_pad, layer):
    tm = _OUTPROJ_TILE
    row = lambda width: pl.BlockSpec((tm, width), lambda i: (i, 0))
    return pl.pallas_call(
        functools.partial(_outproj_kernel, n_src=len(srcs), layer=layer),
        grid=(N_TOK // tm,),
        in_specs=_token_specs(srcs, tm, D) + [
            row(NA_W), row(POOL_W),
            _resident((D, D), lambda i: (0, 0)),
            _mod_spec(layer, 2), _gain_spec(), _mod_spec(layer, 3), _mod_spec(layer, 4),
            _resident((D, LANES), lambda i: (0, 0))],
        out_specs=[row(D), row(D), row(LANES)],
        out_shape=[jax.ShapeDtypeStruct((N_TOK, D), F32),
                   jax.ShapeDtypeStruct((N_TOK, D), BF16),
                   jax.ShapeDtypeStruct((N_TOK, LANES), F32)],
        compiler_params=_cparams(("arbitrary",), 52),
        name="outproj",
    )(*srcs, att, pool, w_out_bf, mods, norm_gain, mods, mods, w_router_pad)


def _choose_slots(aff, n_req, n_tok, cap):
    rows = jnp.concatenate([aff[:, r * n_tok:(r + 1) * n_tok] for r in range(n_req)], axis=0)
    thr = jnp.zeros((n_req * E, 1), jnp.int32)
    for b in range(30, -1, -1):
        cand = thr | (1 << b)
        cnt = jnp.sum((rows >= lax.bitcast_convert_type(cand, F32)).astype(F32), axis=-1, keepdims=True)
        thr = jnp.where(cnt >= cap, cand, thr)
    thr_val = lax.bitcast_convert_type(thr, F32)
    gt = rows > thr_val
    eq = rows == thr_val
    n_gt = jnp.sum(gt.astype(F32), axis=-1, keepdims=True)
    before = (lax.broadcasted_iota(jnp.int32, (n_tok, n_tok), 0)
              < lax.broadcasted_iota(jnp.int32, (n_tok, n_tok), 1)).astype(BF16)
    eq_rank = _dot(eq.astype(BF16), before)
    sel = gt | (eq & (eq_rank < cap - n_gt))
    slot = _dot(sel.astype(BF16), before)
    pieces = [jnp.where(sel[r * E:(r + 1) * E], slot[r * E:(r + 1) * E] + r * cap, -1.0) for r in range(n_req)]
    return jnp.concatenate(pieces, axis=1) if n_req > 1 else pieces[0]


def _route_kernel(lg_ref, pos_ref, aff_ref):
    lg = lg_ref[...].T[0:E, :]
    ex = jnp.exp(lg - jnp.max(lg, axis=0, keepdims=True))
    aff = ex / jnp.sum(ex, axis=0, keepdims=True)
    aff_ref[0] = aff

    @pl.when(pl.program_id(0) < CTX_UNITS)
    def _():
        pos_ref[0] = _choose_slots(aff, REQ_PER_CTX_UNIT, SEQ, CAP_CTX)

    @pl.when(pl.program_id(0) >= CTX_UNITS)
    def _():
        pos_ref[0] = _choose_slots(aff, 1, DEC_SEQ, CAP_LAT)


def _route(logits):
    out_blk = lambda: pl.BlockSpec((1, E, UNIT), lambda u: (u, 0, 0))
    return pl.pallas_call(
        _route_kernel,
        grid=(N_UNITS,),
        in_specs=[pl.BlockSpec((UNIT, LANES), lambda u: (u, 0))],
        out_specs=[out_blk(), out_blk()],
        out_shape=[jax.ShapeDtypeStruct((N_UNITS, E, UNIT), F32)] * 2,
        compiler_params=_cparams(("arbitrary",), 32),
        name="route",
    )(logits)


_COL_CHUNK = 512


def _onehot_rows(pos, p_ref, n_slots, first_slot):
    width = pos.shape[1]
    slot_id = (lax.broadcasted_iota(jnp.int32, (n_slots, width), 0) + first_slot).astype(F32)
    matches = []
    for e in range(E):
        match = pos[e:e + 1, :] == slot_id
        p_ref[e * n_slots:(e + 1) * n_slots, 0:width] = jnp.where(match, 1.0, 0.0).astype(BF16)
        matches.append(match)
    return matches


def _gather_kernel(pos_ref, aff_ref, h_ref, xs_ref, gs_ref, p_ref):
    def gather(tokens, slots, first_slot):
        n = slots.stop - slots.start
        aff = aff_ref[0, :, tokens]
        matches = _onehot_rows(pos_ref[0, :, tokens], p_ref, n, first_slot)
        for e in range(E):
            gate = jnp.sum(jnp.where(matches[e], aff[e:e + 1, :], 0.0), axis=-1, keepdims=True)
            gs_ref[e, slots, :] = jnp.broadcast_to(gate, (n, LANES))
        width = tokens.stop - tokens.start
        for c in range(D // _COL_CHUNK):
            cols = slice(c * _COL_CHUNK, (c + 1) * _COL_CHUNK)
            r = _dot(p_ref[0:E * n, 0:width], h_ref[tokens, cols])
            xs_ref[:, slots, cols] = r.reshape(E, n, _COL_CHUNK).astype(BF16)

    @pl.when(pl.program_id(0) < CTX_UNITS)
    def _():
        for r in range(REQ_PER_CTX_UNIT):
            gather(slice(r * SEQ, (r + 1) * SEQ), slice(r * CAP_CTX, (r + 1) * CAP_CTX), r * CAP_CTX)

    @pl.when(pl.program_id(0) >= CTX_UNITS)
    def _():
        gather(slice(0, UNIT), slice(0, SLOTS), 0)


def _gather(pos, aff, h):
    unit3 = lambda: pl.BlockSpec((1, E, UNIT), lambda u: (u, 0, 0))
    return pl.pallas_call(
        _gather_kernel,
        grid=(N_UNITS,),
        in_specs=[unit3(), unit3(), pl.BlockSpec((UNIT, D), lambda u: (u, 0))],
        out_specs=[pl.BlockSpec((E, SLOTS, D), lambda u: (0, u, 0)),
                   pl.BlockSpec((E, SLOTS, LANES), lambda u: (0, u, 0))],
        out_shape=[jax.ShapeDtypeStruct((E, N_SLOT, D), BF16),
                   jax.ShapeDtypeStruct((E, N_SLOT, LANES), F32)],
        scratch_shapes=[pltpu.VMEM((E * SLOTS, UNIT), BF16)],
        compiler_params=_cparams(("arbitrary",), 48),
        name="gather",
    )(pos, aff, h)


_FFN_ROWS = 256
_FFN_COLS = 256


def _ffn_up_kernel(xs_ref, wg_ref, wu_ref, o_ref):
    for c in range(wg_ref.shape[1] // _FFN_COLS):
        cols = slice(c * _FFN_COLS, (c + 1) * _FFN_COLS)
        wg = wg_ref[:, cols].astype(BF16)
        wu = wu_ref[:, cols].astype(BF16)
        for m in range(N_SLOT // _FFN_ROWS):
            rows = slice(m * _FFN_ROWS, (m + 1) * _FFN_ROWS)
            x = xs_ref[rows, :]
            o_ref[rows, cols] = (_silu(_dot(x, wg)) * _dot(x, wu)).astype(o_ref.dtype)


def _ffn_up(xs, w_gate, w_up, layer):
    tf = 512
    wspec = lambda: pl.BlockSpec((None, None, D, tf), lambda e, j: (layer, e, 0, j))
    return pl.pallas_call(
        _ffn_up_kernel,
        grid=(E, D_EXP // tf),
        in_specs=[pl.BlockSpec((None, N_SLOT, D), lambda e, j: (e, 0, 0)), wspec(), wspec()],
        out_specs=pl.BlockSpec((None, N_SLOT, tf), lambda e, j: (e, 0, j)),
        out_shape=jax.ShapeDtypeStruct((E, N_SLOT, D_EXP), BF16),
        compiler_params=_cparams(("arbitrary", "arbitrary"), 48),
        name="ffn_up",
    )(xs, w_gate, w_up)


def _ffn_down_kernel(h_ref, wd_ref, gs_ref, o_ref):
    for c in range(wd_ref.shape[1] // _FFN_COLS):
        cols = slice(c * _FFN_COLS, (c + 1) * _FFN_COLS)
        wd = wd_ref[:, cols].astype(BF16)
        for m in range(N_SLOT // _FFN_ROWS):
            rows = slice(m * _FFN_ROWS, (m + 1) * _FFN_ROWS)
            gate = jnp.concatenate([gs_ref[rows, :]] * (_FFN_COLS // LANES), axis=1)
            o_ref[rows, cols] = (_dot(h_ref[rows, :], wd) * gate).astype(o_ref.dtype)


def _ffn_down(hcur, w_down, gslot, layer):
    return pl.pallas_call(
        _ffn_down_kernel,
        grid=(E,),
        in_specs=[pl.BlockSpec((None, N_SLOT, D_EXP), lambda e: (e, 0, 0)),
                  pl.BlockSpec((None, None, D_EXP, D), lambda e: (layer, e, 0, 0)),
                  pl.BlockSpec((None, N_SLOT, LANES), lambda e: (e, 0, 0))],
        out_specs=pl.BlockSpec((None, N_SLOT, D), lambda e: (e, 0, 0)),
        out_shape=jax.ShapeDtypeStruct((E, N_SLOT, D), BF16),
        compiler_params=_cparams(("arbitrary",), 48),
        name="ffn_down",
    )(hcur, w_down, gslot)


_SC_TOK = 256


def _scatter_kernel(pos_ref, ys_ref, y_ref, gate_ref, *rest, final):
    if final:
        gain_ref, oc_ref, ol_ref, p_ref = rest
    else:
        o_ref, p_ref = rest

    def emit(updates, is_ctx):
        y = y_ref[...] + _mod_vec(gate_ref, UNIT) * jnp.concatenate(updates, axis=1)
        if not final:
            o_ref[...] = y
        else:
            r = y * lax.rsqrt(jnp.mean(y * y, axis=-1, keepdims=True) + RMS_EPS) * gain_ref[...]
            (oc_ref if is_ctx else ol_ref)[...] = r

    @pl.when(pl.program_id(0) < CTX_UNITS)
    def _():
        first = pl.multiple_of(pl.program_id(1) * CAP_CTX, CAP_CTX)
        _onehot_rows(pos_ref[0], p_ref, CAP_CTX, first)
        emit([_dot_tn(p_ref[0:E * CAP_CTX, :],
                      ys_ref[:, pl.ds(first, CAP_CTX), c * _COL_CHUNK:(c + 1) * _COL_CHUNK]
                      .reshape(E * CAP_CTX, _COL_CHUNK))
              for c in range(D // _COL_CHUNK)], True)

    @pl.when(pl.program_id(0) >= CTX_UNITS)
    def _():
        _onehot_rows(pos_ref[0], p_ref, SLOTS, 0)
        emit([_dot_tn(p_ref[...], ys_ref[:, :, c * _COL_CHUNK:(c + 1) * _COL_CHUNK].reshape(E * SLOTS, _COL_CHUNK))
              for c in range(D // _COL_CHUNK)], False)


def _scatter(pos, ys, y, mods, layer, final_gain=None):
    assert _SC_TOK == SEQ
    per_unit = UNIT // _SC_TOK
    n_ctx_tiles = N_CTX // _SC_TOK
    final = final_gain is not None
    tok = lambda: pl.BlockSpec((_SC_TOK, D), lambda u, s: (u * per_unit + s, 0))
    in_specs = [pl.BlockSpec((1, E, _SC_TOK), lambda u, s: (u, 0, s)),
                pl.BlockSpec((E, SLOTS, D), lambda u, s: (0, u, 0)),
                tok(),
                _mod_spec(layer, 5)]
    args = [pos, ys, y, mods]
    if final:
        in_specs.append(pl.BlockSpec((1, D), lambda u, s: (0, 0)))
        args.append(final_gain.reshape(1, D))
        out_specs = [pl.BlockSpec((_SC_TOK, D), lambda u, s: (jnp.minimum(u * per_unit + s, n_ctx_tiles - 1), 0)),
                     pl.BlockSpec((_SC_TOK, D), lambda u, s: (jnp.maximum(u * per_unit + s - n_ctx_tiles, 0), 0))]
        out_shape = [jax.ShapeDtypeStruct((N_CTX, D), F32), jax.ShapeDtypeStruct((N_LAT, D), F32)]
    else:
        out_specs = tok()
        out_shape = jax.ShapeDtypeStruct((N_TOK, D), F32)
    return pl.pallas_call(
        functools.partial(_scatter_kernel, final=final),
        grid=(N_UNITS, per_unit),
        in_specs=in_specs,
        out_specs=out_specs,
        out_shape=out_shape,
        scratch_shapes=[pltpu.VMEM((E * SLOTS, _SC_TOK), BF16)],
        compiler_params=_cparams(("arbitrary", "arbitrary"), 40),
        name="scatter",
    )(*args)


def kernel(x_prompt, x_sample, cache_k, cache_v, c, c_ctx, w_mod, b_mod, norm_mix, norm_ffn, w_in, rpb,
           w_pool, pool_scale, w_out, w_router, w_gate, w_up, w_down, norm_final):
    srcs = [x_prompt.reshape(N_CTX, D), x_sample.reshape(N_LAT, D)]
    cond = jnp.zeros((N_ROW, D), F32).at[0].set(c_ctx).at[1:1 + DEC_BATCH].set(c)
    mods = _adaln(cond, w_mod, b_mod)
    ck = cache_k.reshape(DEC_BATCH, DEPTH, PAST * H, DH)
    cv = cache_v.reshape(DEC_BATCH, DEPTH, PAST * H, DH)

    new_kv = []
    for l in range(DEPTH):
        q, k, v, u, *new_kv = _inproj(srcs, mods, norm_mix, _to_bf16(w_in, l), l, new_kv)
        att = _attention(q, k, v, ck, cv, _na_bias_table(rpb[l]), l)
        pool = _pool_mixer(u, w_pool, pool_scale, l)

        w_router_pad = jnp.pad(w_router[l], ((0, 0), (0, LANES - E))).astype(BF16)
        y, h, logits = _outproj(srcs, att, pool, mods, norm_ffn, _to_bf16(w_out, l), w_router_pad, l)

        pos, aff = _route(logits)
        xs, gslot = _gather(pos, aff, h)
        ys = _ffn_down(_ffn_up(xs, w_gate, w_up, l), w_down, gslot, l)
        srcs = _scatter(pos, ys, y, mods, l, norm_final if l == DEPTH - 1 else None)
        srcs = list(srcs) if l == DEPTH - 1 else [srcs]

    y_prompt, y_sample = srcs
    new_k, new_v = new_kv
    return (y_prompt.reshape(BATCH, SEQ, D), y_sample.reshape(DEC_BATCH, DEC_SEQ, D),
            new_k.reshape(BATCH, DEPTH, SEQ, H, DH), new_v.reshape(BATCH, DEPTH, SEQ, H, DH))
```

```python
import functools

import jax
import jax.numpy as jnp
from jax import lax
from jax.experimental import pallas as pl
from jax.experimental.pallas import tpu as pltpu

F32 = jnp.float32
BF16 = jnp.bfloat16

D = 2048
BATCH, SEQ = 16, 256
DEPTH = 2
DEC_BATCH, DEC_SEQ = 8, 1024
PAST = 512
GRID_W = 64
GRID_H = DEC_SEQ // GRID_W
H, DH = 16, 64
NA_W = H * DH
POOL_WINDOWS = (2, 4, 8, 16)
PG = 256
POOL_W = 1024
IN_W = 3 * NA_W + POOL_W
NA_ROWS, NA_COLS = 8, 16
E = 16
D_EXP = 1024
N_MOD = 6
RMS_EPS = 1e-6
NEG_INF = -1e30

N_CTX = BATCH * SEQ
N_LAT = DEC_BATCH * DEC_SEQ
N_TOK = N_CTX + N_LAT
UNIT = 1024
N_UNITS = N_TOK // UNIT
CTX_UNITS = N_CTX // UNIT
REQ_PER_CTX_UNIT = UNIT // SEQ
SLOTS = 128
CAP_CTX = 2 * SEQ // E
CAP_LAT = 2 * DEC_SEQ // E
N_SLOT = N_UNITS * SLOTS
N_ROW = 16
LANES = 128
MIB = 1024 * 1024


def _cparams(sem, vmem_mib):
    return pltpu.CompilerParams(dimension_semantics=sem, vmem_limit_bytes=vmem_mib * MIB)


def _resident(block_shape, index_map):
    return pl.BlockSpec(block_shape, index_map, pipeline_mode=pl.Buffered(1))


def _dot(a, b):
    return jnp.dot(a, b, preferred_element_type=F32)


def _dot_nt(a, b):
    return lax.dot_general(a, b, (((1,), (1,)), ((), ())), preferred_element_type=F32)


def _dot_tn(a, b):
    return lax.dot_general(a, b, (((0,), (0,)), ((), ())), preferred_element_type=F32)


def _silu(x):
    return x / (1.0 + jnp.exp(-x))


def _norm_mod(x, gain, shift, scale):
    y = x * lax.rsqrt(jnp.mean(x * x, axis=-1, keepdims=True) + RMS_EPS)
    return (y * gain) * (1.0 + scale) + shift


def _mod_row_of_tile(i, tile):
    per_req = DEC_SEQ // tile
    n_ctx_tiles = N_CTX // tile
    return jnp.where(i < n_ctx_tiles, 0, 1 + (i - n_ctx_tiles) // per_req)


def _mod_spec(layer, which):
    return pl.BlockSpec((None, N_ROW, D), lambda *ids: (layer, 0, which))


def _mod_vec(ref, tile, axis=0):
    return ref[pl.ds(_mod_row_of_tile(pl.program_id(axis), tile), 1), :]


def _gain_spec():
    return pl.BlockSpec((DEPTH, D), lambda *ids: (0, 0))


def _token_specs(srcs, tile, width):
    if len(srcs) == 1:
        return [pl.BlockSpec((tile, width), lambda i: (i, 0))]
    n_ctx_tiles = N_CTX // tile
    return [pl.BlockSpec((tile, width), lambda i: (jnp.minimum(i, n_ctx_tiles - 1), 0)),
            pl.BlockSpec((tile, width), lambda i: (jnp.maximum(i - n_ctx_tiles, 0), 0))]


def _read_tokens(refs, tile, rows=slice(None)):
    if len(refs) == 1:
        return refs[0][rows, :]
    return jnp.where(pl.program_id(0) < N_CTX // tile, refs[0][rows, :], refs[1][rows, :])


def _adaln_kernel(c_ref, w_ref, b_ref, o_ref):
    a = _silu(c_ref[...]).astype(BF16)
    o_ref[0] = _dot(a, w_ref[0].astype(BF16)) + b_ref[pl.ds(pl.program_id(0), 1), :]


def _adaln(cond, w_mod, b_mod):
    tn = 1024
    return pl.pallas_call(
        _adaln_kernel,
        grid=(DEPTH, N_MOD * D // tn),
        in_specs=[pl.BlockSpec((N_ROW, D), lambda l, j: (0, 0)),
                  pl.BlockSpec((1, D, tn), lambda l, j: (l, 0, j)),
                  pl.BlockSpec((DEPTH, tn), lambda l, j: (0, j))],
        out_specs=pl.BlockSpec((1, N_ROW, tn), lambda l, j: (l, 0, j)),
        out_shape=jax.ShapeDtypeStruct((DEPTH, N_ROW, N_MOD * D), F32),
        compiler_params=_cparams(("arbitrary", "arbitrary"), 40),
        name="adaln",
    )(cond, w_mod, b_mod)


def _cast_kernel(w_ref, o_ref):
    o_ref[...] = w_ref[...].astype(BF16)


def _to_bf16(w, layer):
    rows, cols = w.shape[1:]
    tr = 256
    return pl.pallas_call(
        _cast_kernel,
        grid=(rows // tr,),
        in_specs=[pl.BlockSpec((None, tr, cols), lambda i: (layer, i, 0))],
        out_specs=pl.BlockSpec((tr, cols), lambda i: (i, 0)),
        out_shape=jax.ShapeDtypeStruct((rows, cols), BF16),
        compiler_params=_cparams(("arbitrary",), 32),
        name="cast_bf16",
    )(w)


_INPROJ_TILE = 256


def _inproj_kernel(*refs, n_src, n_prev, layer, tm):
    x_refs = refs[:n_src]
    g_ref, sh_ref, sc_ref, w_ref = refs[n_src:n_src + 4]
    q_ref, k_ref, v_ref, u_ref, kf_ref, vf_ref = refs[n_src + 4 + n_prev:]
    for r in range(tm // SEQ):
        rows = slice(r * SEQ, (r + 1) * SEQ)
        h = _norm_mod(_read_tokens(x_refs, tm, rows), g_ref[layer:layer + 1, :], _mod_vec(sh_ref, tm),
                      _mod_vec(sc_ref, tm)).astype(BF16)
        q_ref[rows, :] = (_dot(h, w_ref[:, 0:NA_W]) * (DH ** -0.5)).astype(BF16)
        k = _dot(h, w_ref[:, NA_W:2 * NA_W])
        v = _dot(h, w_ref[:, 2 * NA_W:3 * NA_W])
        k_ref[rows, :] = k.astype(BF16)
        v_ref[rows, :] = v.astype(BF16)
        u_ref[rows, :] = _dot(h, w_ref[:, 3 * NA_W:IN_W])

        @pl.when(pl.program_id(0) < N_CTX // tm)
        def _():
            kf_ref[r, 0] = k
            vf_ref[r, 0] = v


def _inproj(srcs, mods, norm_gain, w_in_bf, layer, kv_prev):
    tm = _INPROJ_TILE if len(srcs) > 1 else 2 * _INPROJ_TILE
    n_ctx_tiles = N_CTX // tm
    tok = lambda: pl.BlockSpec((tm, NA_W), lambda i: (i, 0))
    ctx_only = lambda: pl.BlockSpec((tm // SEQ, 1, SEQ, NA_W),
                                    lambda i: (jnp.minimum(i, n_ctx_tiles - 1), layer, 0, 0))
    n_in = len(srcs) + 4
    return pl.pallas_call(
        functools.partial(_inproj_kernel, n_src=len(srcs), n_prev=len(kv_prev), layer=layer, tm=tm),
        grid=(N_TOK // tm,),
        in_specs=_token_specs(srcs, tm, D) + [
            _gain_spec(), _mod_spec(layer, 0), _mod_spec(layer, 1),
            _resident((D, IN_W), lambda i: (0, 0))] + [pl.BlockSpec(memory_space=pl.ANY)] * len(kv_prev),
        out_specs=[tok(), tok(), tok(), tok(), ctx_only(), ctx_only()],
        out_shape=[jax.ShapeDtypeStruct((N_TOK, NA_W), BF16)] * 3
        + [jax.ShapeDtypeStruct((N_TOK, POOL_W), F32)]
        + [jax.ShapeDtypeStruct((BATCH, DEPTH, SEQ, NA_W), F32)] * 2,
        input_output_aliases={n_in + j: 4 + j for j in range(len(kv_prev))},
        compiler_params=_cparams(("arbitrary",), 56),
        name="inproj",
    )(*srcs, norm_gain, mods, mods, w_in_bf, *kv_prev)


def _na_window_start(qr):
    return min(max(qr - NA_ROWS // 2, 0), GRID_H - NA_ROWS)


def _na_bias_kernel(rpb_ref, o_ref):
    assert LANES == 2 * GRID_W
    qc = lax.broadcasted_iota(jnp.int32, (GRID_W, LANES), 0)
    lane = lax.broadcasted_iota(jnp.int32, (GRID_W, LANES), 1)
    kc = jnp.bitwise_and(lane, GRID_W - 1)
    cs = jnp.clip(qc - NA_COLS // 2, 0, GRID_W - NA_COLS)
    in_col = (kc >= cs) & (kc < cs + NA_COLS)
    keep_lanes = {(True, True): in_col, (True, False): in_col & (lane < GRID_W),
                  (False, True): in_col & (lane >= GRID_W)}
    n_dr = 2 * NA_ROWS - 1
    zero_rows = jnp.zeros((GRID_W, LANES), F32)

    def rpb_rows(dr):
        if not 0 <= dr < n_dr:
            return zero_rows
        return jnp.broadcast_to(rpb_ref[0, dr:dr + 1, :], (GRID_W, LANES))

    pair_cache = {}

    def pair_tile(dr, use_first, use_second):
        if not (use_first or use_second):
            return jnp.full((GRID_W, LANES), NEG_INF, F32)
        key = (dr, use_first, use_second)
        if key not in pair_cache:
            src = rpb_rows(dr) + pltpu.roll(rpb_rows(dr + 1), GRID_W, 1)
            toep = pltpu.roll(src, LANES - (NA_COLS - 1), 1, stride=1, stride_axis=0)
            pair_cache[key] = jnp.where(keep_lanes[(use_first, use_second)], toep, NEG_INF)
        return pair_cache[key]

    for qr in range(GRID_H):
        rs = _na_window_start(qr)
        for kr in range(0, GRID_H, 2):
            use = [rs <= k < rs + NA_ROWS for k in (kr, kr + 1)]
            o_ref[0, qr * GRID_W:(qr + 1) * GRID_W, kr * GRID_W:(kr + 2) * GRID_W] = pair_tile(
                kr - qr + NA_ROWS - 1, use[0], use[1])


def _na_bias_table(rpb_l):
    rpb_pad = jnp.pad(rpb_l, ((0, 0), (0, 0), (0, LANES - rpb_l.shape[-1])))
    n_dr = rpb_l.shape[1]
    return pl.pallas_call(
        _na_bias_kernel,
        grid=(H,),
        in_specs=[pl.BlockSpec((1, n_dr, LANES), lambda h: (h, 0, 0))],
        out_specs=pl.BlockSpec((1, DEC_SEQ, DEC_SEQ), lambda h: (h, 0, 0)),
        out_shape=jax.ShapeDtypeStruct((H, DEC_SEQ, DEC_SEQ), F32),
        compiler_params=_cparams(("arbitrary",), 24),
        name="na_bias",
    )(rpb_pad)


_NA_Q_BLOCK = 4 * GRID_W


def _na_key_range(qb):
    rows_per_block = _NA_Q_BLOCK // GRID_W
    lo = _na_window_start(qb * rows_per_block) * GRID_W
    hi = (_na_window_start((qb + 1) * rows_per_block - 1) + NA_ROWS) * GRID_W
    return (lo // _NA_Q_BLOCK * _NA_Q_BLOCK, -(-hi // _NA_Q_BLOCK) * _NA_Q_BLOCK)


_NA_KEY_RANGES = [_na_key_range(qb) for qb in range(DEC_SEQ // _NA_Q_BLOCK)]


def _attn_kernel(q_ref, k_ref, v_ref, ck_ref, cv_ref, tab_ref, o_ref, kc_ref, vv_ref):
    lane = lax.broadcasted_iota(jnp.int32, (1, 2 * DH), 1)
    head_lanes = [lane < DH, lane >= DH]
    vv_ref[0:UNIT, 0:2 * DH] = v_ref[...]
    vv_ref[:, 2 * DH:4 * DH] = jnp.ones((UNIT + PAST, 2 * DH), BF16)

    def head_queries(rows):
        q = q_ref[rows, :]
        return [jnp.where(m, q, jnp.zeros_like(q)) for m in head_lanes]

    def probs(scores):
        m = functools.reduce(jnp.maximum, [jnp.max(s, axis=-1, keepdims=True) for s in scores])
        return [jnp.exp(s - m).astype(BF16) for s in scores]

    def merge_heads(acc):
        outs = [a[:, 0:2 * DH] / a[:, 2 * DH:4 * DH] for a in acc]
        return jnp.where(head_lanes[0], outs[0], outs[1]).astype(o_ref.dtype)

    @pl.when(pl.program_id(1) < CTX_UNITS)
    def _():
        for r in range(REQ_PER_CTX_UNIT):
            rows = slice(r * SEQ, (r + 1) * SEQ)
            acc = []
            for qh in head_queries(rows):
                (p,) = probs([_dot_nt(qh, k_ref[rows, :])])
                acc.append(_dot(p, vv_ref[rows, :]))
            o_ref[rows, :] = merge_heads(acc)

    @pl.when(pl.program_id(1) >= CTX_UNITS)
    def _():
        kc_ref[...] = ck_ref[0, 0].astype(BF16)
        vv_ref[UNIT:UNIT + PAST, 0:2 * DH] = cv_ref[0, 0].astype(BF16)
        for qb, (k_lo, k_hi) in enumerate(_NA_KEY_RANGES):
            rows = slice(qb * _NA_Q_BLOCK, (qb + 1) * _NA_Q_BLOCK)
            keys = slice(k_lo, k_hi)
            acc = []
            for hh, qh in enumerate(head_queries(rows)):
                s_lat = _dot_nt(qh, k_ref[keys, :]) + tab_ref[hh, rows, keys]
                s_ctx = _dot_nt(qh, kc_ref[...])
                p_lat, p_ctx = probs([s_lat, s_ctx])
                acc.append(_dot(p_lat, vv_ref[keys, :]) + _dot(p_ctx, vv_ref[UNIT:UNIT + PAST, :]))
            o_ref[rows, :] = merge_heads(acc)


def _attention(q, k, v, cache_k, cache_v, tab, layer):
    blk = lambda: pl.BlockSpec((UNIT, 2 * DH), lambda g, u: (u, g))
    cache = lambda: pl.BlockSpec((1, 1, PAST, 2 * DH), lambda g, u: (jnp.maximum(u - CTX_UNITS, 0), layer, 0, g))
    return pl.pallas_call(
        _attn_kernel,
        grid=(H // 2, N_UNITS),
        in_specs=[blk(), blk(), blk(), cache(), cache(),
                  pl.BlockSpec((2, DEC_SEQ, DEC_SEQ), lambda g, u: (g, 0, 0))],
        out_specs=blk(),
        out_shape=jax.ShapeDtypeStruct((N_TOK, NA_W), BF16),
        scratch_shapes=[pltpu.VMEM((PAST, 2 * DH), BF16),
                        pltpu.VMEM((UNIT + PAST, 4 * DH), BF16)],
        compiler_params=_cparams(("arbitrary", "arbitrary"), 44),
        name="attention",
    )(q, k, v, cache_k, cache_v, tab)


def _pool_kernel(u_ref, wp_ref, ps_ref, o_ref, *, layer):
    n = jnp.where(pl.program_id(0) < CTX_UNITS, SEQ, DEC_SEQ)
    t = lax.broadcasted_iota(jnp.int32, (UNIT, PG), 0)
    p = jnp.bitwise_and(t, n - 1)

    def fwd(a, s):
        return jnp.where(p < n - s, pltpu.roll(a, UNIT - s, 0), 0.0)

    def bwd(a, s):
        return jnp.where(p >= s, pltpu.roll(a, s, 0), 0.0)

    for gi, w in enumerate(POOL_WINDOWS):
        half = w // 2
        cols = slice(gi * PG, (gi + 1) * PG)
        g = u_ref[:, cols]
        f, b, s = g, g, 1
        while s < half:
            f = f + fwd(f, s)
            b = b + bwd(b, s)
            s *= 2
        tot = f + bwd(b, 1)
        cnt = (jnp.minimum(p + half, n) - jnp.maximum(p - half, 0)).astype(F32)
        dlt = tot / cnt - g
        out = _dot(dlt.astype(BF16), wp_ref[gi].astype(BF16)) * ps_ref[layer:layer + 1, cols]
        o_ref[:, cols] = out.astype(o_ref.dtype)


def _pool_mixer(u, w_pool, pool_scale, layer):
    return pl.pallas_call(
        functools.partial(_pool_kernel, layer=layer),
        grid=(N_UNITS,),
        in_specs=[pl.BlockSpec((UNIT, POOL_W), lambda i: (i, 0)),
                  pl.BlockSpec((None, len(POOL_WINDOWS), PG, PG), lambda i: (layer, 0, 0, 0)),
                  pl.BlockSpec((DEPTH, POOL_W), lambda i: (0, 0))],
        out_specs=pl.BlockSpec((UNIT, POOL_W), lambda i: (i, 0)),
        out_shape=jax.ShapeDtypeStruct((N_TOK, POOL_W), BF16),
        compiler_params=_cparams(("arbitrary",), 40),
        name="pool",
    )(u, w_pool, pool_scale)


_OUTPROJ_TILE = 512
_OUTPROJ_ROWS = 256


def _outproj_kernel(*refs, n_src, layer):
    x_refs = refs[:n_src]
    att_ref, pool_ref, w_ref, gate_ref, g2_ref, sh_ref, sc_ref, wr_ref, y_ref, h_ref, lg_ref = refs[n_src:]
    tm = _OUTPROJ_TILE
    for r in range(tm // _OUTPROJ_ROWS):
        rows = slice(r * _OUTPROJ_ROWS, (r + 1) * _OUTPROJ_ROWS)
        mix = _dot(att_ref[rows, :], w_ref[0:NA_W, :]) + _dot(pool_ref[rows, :], w_ref[NA_W:NA_W + POOL_W, :])
        y = _read_tokens(x_refs, tm, rows) + _mod_vec(gate_ref, tm) * mix
        y_ref[rows, :] = y
        h = _norm_mod(y, g2_ref[layer:layer + 1, :], _mod_vec(sh_ref, tm), _mod_vec(sc_ref, tm)).astype(BF16)
        h_ref[rows, :] = h
        lg_ref[rows, :] = _dot(h, wr_ref[...])


def _outproj(srcs, att, pool, mods, norm_gain, w_out_bf, w_router_pad, layer):
    tm = _OUTPROJ_TILE
    row = lambda width: pl.BlockSpec((tm, width), lambda i: (i, 0))
    return pl.pallas_call(
        functools.partial(_outproj_kernel, n_src=len(srcs), layer=layer),
        grid=(N_TOK // tm,),
        in_specs=_token_specs(srcs, tm, D) + [
            row(NA_W), row(POOL_W),
            _resident((D, D), lambda i: (0, 0)),
            _mod_spec(layer, 2), _gain_spec(), _mod_spec(layer, 3), _mod_spec(layer, 4),
            _resident((D, LANES), lambda i: (0, 0))],
        out_specs=[row(D), row(D), row(LANES)],
        out_shape=[jax.ShapeDtypeStruct((N_TOK, D), F32),
                   jax.ShapeDtypeStruct((N_TOK, D), BF16),
                   jax.ShapeDtypeStruct((N_TOK, LANES), F32)],
        compiler_params=_cparams(("arbitrary",), 52),
        name="outproj",
    )(*srcs, att, pool, w_out_bf, mods, norm_gain, mods, mods, w_router_pad)


_SEARCH_BITS = 3


def _choose_slots(aff, n_req, n_tok, cap):
    rows = jnp.concatenate([aff[:, r * n_tok:(r + 1) * n_tok] for r in range(n_req)], axis=0)
    n_rows = n_req * E
    thr = jnp.zeros((n_rows, 1), jnp.int32)
    hi = 31
    while hi > 0:
        lo = max(hi - _SEARCH_BITS, 0)
        n_cand = (1 << (hi - lo)) - 1
        hits = jnp.concatenate([(rows >= lax.bitcast_convert_type(thr | (j << lo), F32)).astype(F32)
                                for j in range(1, n_cand + 1)], axis=0)
        reached = (jnp.sum(hits, axis=-1, keepdims=True) >= cap).astype(jnp.int32)
        group = functools.reduce(jnp.add, [reached[j * n_rows:(j + 1) * n_rows] for j in range(n_cand)])
        thr = thr | lax.shift_left(group, jnp.int32(lo))
        hi = lo
    thr_val = lax.bitcast_convert_type(thr, F32)
    gt = rows > thr_val
    eq = rows == thr_val
    n_gt = jnp.sum(gt.astype(F32), axis=-1, keepdims=True)
    before = (lax.broadcasted_iota(jnp.int32, (n_tok, n_tok), 0)
              < lax.broadcasted_iota(jnp.int32, (n_tok, n_tok), 1)).astype(BF16)
    eq_rank = _dot(eq.astype(BF16), before)
    sel = gt | (eq & (eq_rank < cap - n_gt))
    slot = _dot(sel.astype(BF16), before)
    pieces = [jnp.where(sel[r * E:(r + 1) * E], slot[r * E:(r + 1) * E] + r * cap, -1.0) for r in range(n_req)]
    return jnp.concatenate(pieces, axis=1) if n_req > 1 else pieces[0]


def _route_kernel(lg_ref, pos_ref, aff_ref):
    lg = lg_ref[...].T[0:E, :]
    ex = jnp.exp(lg - jnp.max(lg, axis=0, keepdims=True))
    aff = ex / jnp.sum(ex, axis=0, keepdims=True)
    aff_ref[0] = aff

    @pl.when(pl.program_id(0) < CTX_UNITS)
    def _():
        pos_ref[0] = _choose_slots(aff, REQ_PER_CTX_UNIT, SEQ, CAP_CTX)

    @pl.when(pl.program_id(0) >= CTX_UNITS)
    def _():
        pos_ref[0] = _choose_slots(aff, 1, DEC_SEQ, CAP_LAT)


def _route(logits):
    out_blk = lambda: pl.BlockSpec((1, E, UNIT), lambda u: (u, 0, 0))
    return pl.pallas_call(
        _route_kernel,
        grid=(N_UNITS,),
        in_specs=[pl.BlockSpec((UNIT, LANES), lambda u: (u, 0))],
        out_specs=[out_blk(), out_blk()],
        out_shape=[jax.ShapeDtypeStruct((N_UNITS, E, UNIT), F32)] * 2,
        compiler_params=_cparams(("arbitrary",), 32),
        name="route",
    )(logits)


_COL_CHUNK = 512


def _onehot_rows(pos, p_ref, n_slots, first_slot):
    width = pos.shape[1]
    slot_id = (lax.broadcasted_iota(jnp.int32, (n_slots, width), 0) + first_slot).astype(F32)
    matches = []
    for e in range(E):
        match = pos[e:e + 1, :] == slot_id
        p_ref[e * n_slots:(e + 1) * n_slots, 0:width] = jnp.where(match, 1.0, 0.0).astype(BF16)
        matches.append(match)
    return matches


def _gather_kernel(pos_ref, aff_ref, h_ref, xs_ref, gs_ref, p_ref):
    def gather(tokens, slots, first_slot):
        n = slots.stop - slots.start
        aff = aff_ref[0, :, tokens]
        matches = _onehot_rows(pos_ref[0, :, tokens], p_ref, n, first_slot)
        for e in range(E):
            gate = jnp.sum(jnp.where(matches[e], aff[e:e + 1, :], 0.0), axis=-1, keepdims=True)
            gs_ref[e, slots, :] = jnp.broadcast_to(gate, (n, LANES))
        width = tokens.stop - tokens.start
        for c in range(D // _COL_CHUNK):
            cols = slice(c * _COL_CHUNK, (c + 1) * _COL_CHUNK)
            r = _dot(p_ref[0:E * n, 0:width], h_ref[tokens, cols])
            xs_ref[:, slots, cols] = r.reshape(E, n, _COL_CHUNK).astype(BF16)

    @pl.when(pl.program_id(0) < CTX_UNITS)
    def _():
        for r in range(REQ_PER_CTX_UNIT):
            gather(slice(r * SEQ, (r + 1) * SEQ), slice(r * CAP_CTX, (r + 1) * CAP_CTX), r * CAP_CTX)

    @pl.when(pl.program_id(0) >= CTX_UNITS)
    def _():
        gather(slice(0, UNIT), slice(0, SLOTS), 0)


def _gather(pos, aff, h):
    unit3 = lambda: pl.BlockSpec((1, E, UNIT), lambda u: (u, 0, 0))
    return pl.pallas_call(
        _gather_kernel,
        grid=(N_UNITS,),
        in_specs=[unit3(), unit3(), pl.BlockSpec((UNIT, D), lambda u: (u, 0))],
        out_specs=[pl.BlockSpec((E, SLOTS, D), lambda u: (0, u, 0)),
                   pl.BlockSpec((E, SLOTS, LANES), lambda u: (0, u, 0))],
        out_shape=[jax.ShapeDtypeStruct((E, N_SLOT, D), BF16),
                   jax.ShapeDtypeStruct((E, N_SLOT, LANES), F32)],
        scratch_shapes=[pltpu.VMEM((E * SLOTS, UNIT), BF16)],
        compiler_params=_cparams(("arbitrary",), 48),
        name="gather",
    )(pos, aff, h)


_FFN_ROWS = 256
_FFN_COLS = 256


def _ffn_up_kernel(xs_ref, wg_ref, wu_ref, o_ref):
    for c in range(wg_ref.shape[1] // _FFN_COLS):
        cols = slice(c * _FFN_COLS, (c + 1) * _FFN_COLS)
        wg = wg_ref[:, cols].astype(BF16)
        wu = wu_ref[:, cols].astype(BF16)
        for m in range(N_SLOT // _FFN_ROWS):
            rows = slice(m * _FFN_ROWS, (m + 1) * _FFN_ROWS)
            x = xs_ref[rows, :]
            o_ref[rows, cols] = (_silu(_dot(x, wg)) * _dot(x, wu)).astype(o_ref.dtype)


def _ffn_up(xs, w_gate, w_up, layer):
    tf = 512
    wspec = lambda: pl.BlockSpec((None, None, D, tf), lambda e, j: (layer, e, 0, j))
    return pl.pallas_call(
        _ffn_up_kernel,
        grid=(E, D_EXP // tf),
        in_specs=[pl.BlockSpec((None, N_SLOT, D), lambda e, j: (e, 0, 0)), wspec(), wspec()],
        out_specs=pl.BlockSpec((None, N_SLOT, tf), lambda e, j: (e, 0, j)),
        out_shape=jax.ShapeDtypeStruct((E, N_SLOT, D_EXP), BF16),
        compiler_params=_cparams(("arbitrary", "arbitrary"), 48),
        name="ffn_up",
    )(xs, w_gate, w_up)


def _ffn_down_kernel(h_ref, wd_ref, gs_ref, o_ref):
    for c in range(wd_ref.shape[1] // _FFN_COLS):
        cols = slice(c * _FFN_COLS, (c + 1) * _FFN_COLS)
        wd = wd_ref[:, cols].astype(BF16)
        for m in range(N_SLOT // _FFN_ROWS):
            rows = slice(m * _FFN_ROWS, (m + 1) * _FFN_ROWS)
            gate = jnp.concatenate([gs_ref[rows, :]] * (_FFN_COLS // LANES), axis=1)
            o_ref[rows, cols] = (_dot(h_ref[rows, :], wd) * gate).astype(o_ref.dtype)


def _ffn_down(hcur, w_down, gslot, layer):
    return pl.pallas_call(
        _ffn_down_kernel,
        grid=(E,),
        in_specs=[pl.BlockSpec((None, N_SLOT, D_EXP), lambda e: (e, 0, 0)),
                  pl.BlockSpec((None, None, D_EXP, D), lambda e: (layer, e, 0, 0)),
                  pl.BlockSpec((None, N_SLOT, LANES), lambda e: (e, 0, 0))],
        out_specs=pl.BlockSpec((None, N_SLOT, D), lambda e: (e, 0, 0)),
        out_shape=jax.ShapeDtypeStruct((E, N_SLOT, D), BF16),
        compiler_params=_cparams(("arbitrary",), 48),
        name="ffn_down",
    )(hcur, w_down, gslot)


_SC_TOK = 256


def _scatter_kernel(pos_ref, ys_ref, y_ref, gate_ref, *rest, final):
    if final:
        gain_ref, oc_ref, ol_ref, p_ref = rest
    else:
        o_ref, p_ref = rest

    def emit(updates, is_ctx):
        y = y_ref[...] + _mod_vec(gate_ref, UNIT) * jnp.concatenate(updates, axis=1)
        if not final:
            o_ref[...] = y
        else:
            r = y * lax.rsqrt(jnp.mean(y * y, axis=-1, keepdims=True) + RMS_EPS) * gain_ref[...]
            (oc_ref if is_ctx else ol_ref)[...] = r

    @pl.when(pl.program_id(0) < CTX_UNITS)
    def _():
        first = pl.multiple_of(pl.program_id(1) * CAP_CTX, CAP_CTX)
        _onehot_rows(pos_ref[0], p_ref, CAP_CTX, first)
        emit([_dot_tn(p_ref[0:E * CAP_CTX, :],
                      ys_ref[:, pl.ds(first, CAP_CTX), c * _COL_CHUNK:(c + 1) * _COL_CHUNK]
                      .reshape(E * CAP_CTX, _COL_CHUNK))
              for c in range(D // _COL_CHUNK)], True)

    @pl.when(pl.program_id(0) >= CTX_UNITS)
    def _():
        _onehot_rows(pos_ref[0], p_ref, SLOTS, 0)
        emit([_dot_tn(p_ref[...], ys_ref[:, :, c * _COL_CHUNK:(c + 1) * _COL_CHUNK].reshape(E * SLOTS, _COL_CHUNK))
              for c in range(D // _COL_CHUNK)], False)


def _scatter(pos, ys, y, mods, layer, final_gain=None):
    assert _SC_TOK == SEQ
    per_unit = UNIT // _SC_TOK
    n_ctx_tiles = N_CTX // _SC_TOK
    final = final_gain is not None
    tok = lambda: pl.BlockSpec((_SC_TOK, D), lambda u, s: (u * per_unit + s, 0))
    in_specs = [pl.BlockSpec((1, E, _SC_TOK), lambda u, s: (u, 0, s)),
                pl.BlockSpec((E, SLOTS, D), lambda u, s: (0, u, 0)),
                tok(),
                _mod_spec(layer, 5)]
    args = [pos, ys, y, mods]
    if final:
        in_specs.append(pl.BlockSpec((1, D), lambda u, s: (0, 0)))
        args.append(final_gain.reshape(1, D))
        out_specs = [pl.BlockSpec((_SC_TOK, D), lambda u, s: (jnp.minimum(u * per_unit + s, n_ctx_tiles - 1), 0)),
                     pl.BlockSpec((_SC_TOK, D), lambda u, s: (jnp.maximum(u * per_unit + s - n_ctx_tiles, 0), 0))]
        out_shape = [jax.ShapeDtypeStruct((N_CTX, D), F32), jax.ShapeDtypeStruct((N_LAT, D), F32)]
    else:
        out_specs = tok()
        out_shape = jax.ShapeDtypeStruct((N_TOK, D), F32)
    return pl.pallas_call(
        functools.partial(_scatter_kernel, final=final),
        grid=(N_UNITS, per_unit),
        in_specs=in_specs,
        out_specs=out_specs,
        out_shape=out_shape,
        scratch_shapes=[pltpu.VMEM((E * SLOTS, _SC_TOK), BF16)],
        compiler_params=_cparams(("arbitrary", "arbitrary"), 40),
        name="scatter",
    )(*args)


def kernel(x_prompt, x_sample, cache_k, cache_v, c, c_ctx, w_mod, b_mod, norm_mix, norm_ffn, w_in, rpb,
           w_pool, pool_scale, w_out, w_router, w_gate, w_up, w_down, norm_final):
    srcs = [x_prompt.reshape(N_CTX, D), x_sample.reshape(N_LAT, D)]
    cond = jnp.zeros((N_ROW, D), F32).at[0].set(c_ctx).at[1:1 + DEC_BATCH].set(c)
    mods = _adaln(cond, w_mod, b_mod)
    ck = cache_k.reshape(DEC_BATCH, DEPTH, PAST, NA_W)
    cv = cache_v.reshape(DEC_BATCH, DEPTH, PAST, NA_W)

    new_kv = []
    for l in range(DEPTH):
        q, k, v, u, *new_kv = _inproj(srcs, mods, norm_mix, _to_bf16(w_in, l), l, new_kv)
        att = _attention(q, k, v, ck, cv, _na_bias_table(rpb[l]), l)
        pool = _pool_mixer(u, w_pool, pool_scale, l)

        w_router_pad = jnp.pad(w_router[l], ((0, 0), (0, LANES - E))).astype(BF16)
        y, h, logits = _outproj(srcs, att, pool, mods, norm_ffn, _to_bf16(w_out, l), w_router_pad, l)

        pos, aff = _route(logits)
        xs, gslot = _gather(pos, aff, h)
        ys = _ffn_down(_ffn_up(xs, w_gate, w_up, l), w_down, gslot, l)
        srcs = _scatter(pos, ys, y, mods, l, norm_final if l == DEPTH - 1 else None)
        srcs = list(srcs) if l == DEPTH - 1 else [srcs]

    y_prompt, y_sample = srcs
    new_k, new_v = new_kv
    return (y_prompt.reshape(BATCH, SEQ, D), y_sample.reshape(DEC_BATCH, DEC_SEQ, D),
            new_k.reshape(BATCH, DEPTH, SEQ, H, DH), new_v.reshape(BATCH, DEPTH, SEQ, H, DH))
```

```python
import functools

import jax
import jax.numpy as jnp
from jax import lax
from jax.experimental import pallas as pl
from jax.experimental.pallas import tpu as pltpu

F32 = jnp.float32
BF16 = jnp.bfloat16

D = 2048
BATCH, SEQ = 16, 256
DEPTH = 2
DEC_BATCH, DEC_SEQ = 8, 1024
PAST = 512
GRID_W = 64
GRID_H = DEC_SEQ // GRID_W
H, DH = 16, 64
NA_W = H * DH
POOL_WINDOWS = (2, 4, 8, 16)
PG = 256
POOL_W = 1024
IN_W = 3 * NA_W + POOL_W
NA_ROWS, NA_COLS = 8, 16
E = 16
D_EXP = 1024
N_MOD = 6
RMS_EPS = 1e-6
NEG_INF = -1e30

N_CTX = BATCH * SEQ
N_LAT = DEC_BATCH * DEC_SEQ
N_TOK = N_CTX + N_LAT
UNIT = 1024
N_UNITS = N_TOK // UNIT
CTX_UNITS = N_CTX // UNIT
REQ_PER_CTX_UNIT = UNIT // SEQ
SLOTS = 128
CAP_CTX = 2 * SEQ // E
CAP_LAT = 2 * DEC_SEQ // E
N_SLOT = N_UNITS * SLOTS
N_ROW = 16
LANES = 128
MIB = 1024 * 1024


def _cparams(sem, vmem_mib):
    return pltpu.CompilerParams(dimension_semantics=sem, vmem_limit_bytes=vmem_mib * MIB)


def _resident(block_shape, index_map):
    return pl.BlockSpec(block_shape, index_map, pipeline_mode=pl.Buffered(1))


def _dot(a, b):
    return jnp.dot(a, b, preferred_element_type=F32)


def _dot_nt(a, b):
    return lax.dot_general(a, b, (((1,), (1,)), ((), ())), preferred_element_type=F32)


def _dot_tn(a, b):
    return lax.dot_general(a, b, (((0,), (0,)), ((), ())), preferred_element_type=F32)


def _silu(x):
    return x / (1.0 + jnp.exp(-x))


def _norm_mod(x, gain, shift, scale):
    y = x * lax.rsqrt(jnp.mean(x * x, axis=-1, keepdims=True) + RMS_EPS)
    return (y * gain) * (1.0 + scale) + shift


def _mod_row_of_tile(i, tile):
    per_req = DEC_SEQ // tile
    n_ctx_tiles = N_CTX // tile
    return jnp.where(i < n_ctx_tiles, 0, 1 + (i - n_ctx_tiles) // per_req)


def _mod_spec(layer, which):
    return pl.BlockSpec((None, N_ROW, D), lambda *ids: (layer, 0, which))


def _mod_vec(ref, tile, axis=0):
    return ref[pl.ds(_mod_row_of_tile(pl.program_id(axis), tile), 1), :]


def _gain_spec():
    return pl.BlockSpec((DEPTH, D), lambda *ids: (0, 0))


def _token_specs(srcs, tile, width):
    if len(srcs) == 1:
        return [pl.BlockSpec((tile, width), lambda i: (i, 0))]
    n_ctx_tiles = N_CTX // tile
    return [pl.BlockSpec((tile, width), lambda i: (jnp.minimum(i, n_ctx_tiles - 1), 0)),
            pl.BlockSpec((tile, width), lambda i: (jnp.maximum(i - n_ctx_tiles, 0), 0))]


def _read_tokens(refs, tile, rows=slice(None)):
    if len(refs) == 1:
        return refs[0][rows, :]
    return jnp.where(pl.program_id(0) < N_CTX // tile, refs[0][rows, :], refs[1][rows, :])


def _adaln_kernel(c_ref, w_ref, b_ref, o_ref):
    a = _silu(c_ref[...]).astype(BF16)
    o_ref[0] = _dot(a, w_ref[0].astype(BF16)) + b_ref[pl.ds(pl.program_id(0), 1), :]


def _adaln(cond, w_mod, b_mod):
    tn = 1024
    return pl.pallas_call(
        _adaln_kernel,
        grid=(DEPTH, N_MOD * D // tn),
        in_specs=[pl.BlockSpec((N_ROW, D), lambda l, j: (0, 0)),
                  pl.BlockSpec((1, D, tn), lambda l, j: (l, 0, j)),
                  pl.BlockSpec((DEPTH, tn), lambda l, j: (0, j))],
        out_specs=pl.BlockSpec((1, N_ROW, tn), lambda l, j: (l, 0, j)),
        out_shape=jax.ShapeDtypeStruct((DEPTH, N_ROW, N_MOD * D), F32),
        compiler_params=_cparams(("arbitrary", "arbitrary"), 40),
        name="adaln",
    )(cond, w_mod, b_mod)


def _cast_kernel(w_ref, o_ref):
    o_ref[...] = w_ref[...].astype(BF16)


def _to_bf16(w, layer):
    rows, cols = w.shape[1:]
    tr = 256
    return pl.pallas_call(
        _cast_kernel,
        grid=(rows // tr,),
        in_specs=[pl.BlockSpec((None, tr, cols), lambda i: (layer, i, 0))],
        out_specs=pl.BlockSpec((tr, cols), lambda i: (i, 0)),
        out_shape=jax.ShapeDtypeStruct((rows, cols), BF16),
        compiler_params=_cparams(("arbitrary",), 32),
        name="cast_bf16",
    )(w)


_INPROJ_TILE = 256


def _inproj_kernel(*refs, n_src, n_prev, layer, tm):
    x_refs = refs[:n_src]
    g_ref, sh_ref, sc_ref, w_ref = refs[n_src:n_src + 4]
    q_ref, k_ref, v_ref, u_ref, kf_ref, vf_ref = refs[n_src + 4 + n_prev:]
    for r in range(tm // SEQ):
        rows = slice(r * SEQ, (r + 1) * SEQ)
        h = _norm_mod(_read_tokens(x_refs, tm, rows), g_ref[layer:layer + 1, :], _mod_vec(sh_ref, tm),
                      _mod_vec(sc_ref, tm)).astype(BF16)
        q_ref[rows, :] = (_dot(h, w_ref[:, 0:NA_W]) * (DH ** -0.5)).astype(BF16)
        k = _dot(h, w_ref[:, NA_W:2 * NA_W])
        v = _dot(h, w_ref[:, 2 * NA_W:3 * NA_W])
        k_ref[rows, :] = k.astype(BF16)
        v_ref[rows, :] = v.astype(BF16)
        u_ref[rows, :] = _dot(h, w_ref[:, 3 * NA_W:IN_W])

        @pl.when(pl.program_id(0) < N_CTX // tm)
        def _():
            kf_ref[r, 0] = k
            vf_ref[r, 0] = v


def _inproj(srcs, mods, norm_gain, w_in_bf, layer, kv_prev):
    tm = _INPROJ_TILE if len(srcs) > 1 else 2 * _INPROJ_TILE
    n_ctx_tiles = N_CTX // tm
    tok = lambda: pl.BlockSpec((tm, NA_W), lambda i: (i, 0))
    ctx_only = lambda: pl.BlockSpec((tm // SEQ, 1, SEQ, NA_W),
                                    lambda i: (jnp.minimum(i, n_ctx_tiles - 1), layer, 0, 0))
    n_in = len(srcs) + 4
    return pl.pallas_call(
        functools.partial(_inproj_kernel, n_src=len(srcs), n_prev=len(kv_prev), layer=layer, tm=tm),
        grid=(N_TOK // tm,),
        in_specs=_token_specs(srcs, tm, D) + [
            _gain_spec(), _mod_spec(layer, 0), _mod_spec(layer, 1),
            _resident((D, IN_W), lambda i: (0, 0))] + [pl.BlockSpec(memory_space=pl.ANY)] * len(kv_prev),
        out_specs=[tok(), tok(), tok(), tok(), ctx_only(), ctx_only()],
        out_shape=[jax.ShapeDtypeStruct((N_TOK, NA_W), BF16)] * 3
        + [jax.ShapeDtypeStruct((N_TOK, POOL_W), F32)]
        + [jax.ShapeDtypeStruct((BATCH, DEPTH, SEQ, NA_W), F32)] * 2,
        input_output_aliases={n_in + j: 4 + j for j in range(len(kv_prev))},
        compiler_params=_cparams(("arbitrary",), 56),
        name="inproj",
    )(*srcs, norm_gain, mods, mods, w_in_bf, *kv_prev)


def _na_window_start(qr):
    return min(max(qr - NA_ROWS // 2, 0), GRID_H - NA_ROWS)


def _fill_na_bias(rpb_ref, o_ref):
    assert LANES == 2 * GRID_W
    qc = lax.broadcasted_iota(jnp.int32, (GRID_W, LANES), 0)
    lane = lax.broadcasted_iota(jnp.int32, (GRID_W, LANES), 1)
    kc = jnp.bitwise_and(lane, GRID_W - 1)
    cs = jnp.clip(qc - NA_COLS // 2, 0, GRID_W - NA_COLS)
    in_col = (kc >= cs) & (kc < cs + NA_COLS)
    keep_lanes = {(True, True): in_col, (True, False): in_col & (lane < GRID_W),
                  (False, True): in_col & (lane >= GRID_W)}
    n_dr = 2 * NA_ROWS - 1
    zero_rows = jnp.zeros((GRID_W, LANES), F32)

    def rpb_rows(dr):
        if not 0 <= dr < n_dr:
            return zero_rows
        return jnp.broadcast_to(rpb_ref[dr:dr + 1, :], (GRID_W, LANES))

    pair_cache = {}

    def pair_tile(dr, use_first, use_second):
        if not (use_first or use_second):
            return jnp.full((GRID_W, LANES), NEG_INF, F32)
        key = (dr, use_first, use_second)
        if key not in pair_cache:
            src = rpb_rows(dr) + pltpu.roll(rpb_rows(dr + 1), GRID_W, 1)
            toep = pltpu.roll(src, LANES - (NA_COLS - 1), 1, stride=1, stride_axis=0)
            pair_cache[key] = jnp.where(keep_lanes[(use_first, use_second)], toep, NEG_INF)
        return pair_cache[key]

    for qr in range(GRID_H):
        rs = _na_window_start(qr)
        for kr in range(0, GRID_H, 2):
            use = [rs <= k < rs + NA_ROWS for k in (kr, kr + 1)]
            o_ref[qr * GRID_W:(qr + 1) * GRID_W, kr * GRID_W:(kr + 2) * GRID_W] = pair_tile(
                kr - qr + NA_ROWS - 1, use[0], use[1])


_NA_Q_BLOCK = 4 * GRID_W


def _na_key_range(qb):
    rows_per_block = _NA_Q_BLOCK // GRID_W
    lo = _na_window_start(qb * rows_per_block) * GRID_W
    hi = (_na_window_start((qb + 1) * rows_per_block - 1) + NA_ROWS) * GRID_W
    return (lo // _NA_Q_BLOCK * _NA_Q_BLOCK, -(-hi // _NA_Q_BLOCK) * _NA_Q_BLOCK)


_NA_KEY_RANGES = [_na_key_range(qb) for qb in range(DEC_SEQ // _NA_Q_BLOCK)]


_PAIRS_PER_STEP = 2


def _attn_kernel(q_ref, k_ref, v_ref, ck_ref, cv_ref, rpb_ref, o_ref, kc_ref, vv_ref, tab_ref):
    @pl.when(pl.program_id(1) == CTX_UNITS)
    def _():
        for h in range(2 * _PAIRS_PER_STEP):
            _fill_na_bias(rpb_ref.at[h], tab_ref.at[h])

    lane = lax.broadcasted_iota(jnp.int32, (1, 2 * DH), 1)
    head_lanes = [lane < DH, lane >= DH]

    def probs(scores):
        m = functools.reduce(jnp.maximum, [jnp.max(s, axis=-1, keepdims=True) for s in scores])
        return [jnp.exp(s - m).astype(BF16) for s in scores]

    def merge_heads(acc):
        outs = [a[:, 0:2 * DH] / a[:, 2 * DH:4 * DH] for a in acc]
        return jnp.where(head_lanes[0], outs[0], outs[1]).astype(o_ref.dtype)

    pairs = [(pp, slice(pp * 2 * DH, (pp + 1) * 2 * DH)) for pp in range(_PAIRS_PER_STEP)]
    for pp, pair in pairs:
        vv_ref[pp, 0:UNIT, 0:2 * DH] = v_ref[:, pair]
        vv_ref[pp, :, 2 * DH:4 * DH] = jnp.ones((UNIT + PAST, 2 * DH), BF16)

    def head_queries(rows, pair):
        q = q_ref[rows, pair]
        return [jnp.where(m, q, jnp.zeros_like(q)) for m in head_lanes]

    @pl.when(pl.program_id(1) < CTX_UNITS)
    def _():
        for pp, pair in pairs:
            for r in range(REQ_PER_CTX_UNIT):
                rows = slice(r * SEQ, (r + 1) * SEQ)
                acc = []
                for qh in head_queries(rows, pair):
                    (p,) = probs([_dot_nt(qh, k_ref[rows, pair])])
                    acc.append(_dot(p, vv_ref[pp, rows, :]))
                o_ref[rows, pair] = merge_heads(acc)

    @pl.when(pl.program_id(1) >= CTX_UNITS)
    def _():
        for pp, pair in pairs:
            kc_ref[pp] = ck_ref[0, 0, :, pair].astype(BF16)
            vv_ref[pp, UNIT:UNIT + PAST, 0:2 * DH] = cv_ref[0, 0, :, pair].astype(BF16)
            for qb, (k_lo, k_hi) in enumerate(_NA_KEY_RANGES):
                rows = slice(qb * _NA_Q_BLOCK, (qb + 1) * _NA_Q_BLOCK)
                keys = slice(k_lo, k_hi)
                acc = []
                for hh, qh in enumerate(head_queries(rows, pair)):
                    s_lat = _dot_nt(qh, k_ref[keys, pair]) + tab_ref[2 * pp + hh, rows, keys]
                    s_ctx = _dot_nt(qh, kc_ref[pp])
                    p_lat, p_ctx = probs([s_lat, s_ctx])
                    acc.append(_dot(p_lat, vv_ref[pp, keys, :]) + _dot(p_ctx, vv_ref[pp, UNIT:UNIT + PAST, :]))
                o_ref[rows, pair] = merge_heads(acc)


def _attention(q, k, v, cache_k, cache_v, rpb, layer):
    width = _PAIRS_PER_STEP * 2 * DH
    blk = lambda: pl.BlockSpec((UNIT, width), lambda g, u: (u, g))
    cache = lambda: pl.BlockSpec((1, 1, PAST, width), lambda g, u: (jnp.maximum(u - CTX_UNITS, 0), layer, 0, g))
    rpb_pad = jnp.pad(rpb, ((0, 0), (0, 0), (0, 0), (0, LANES - rpb.shape[-1])))
    return pl.pallas_call(
        _attn_kernel,
        grid=(NA_W // width, N_UNITS),
        in_specs=[blk(), blk(), blk(), cache(), cache(),
                  pl.BlockSpec((None, 2 * _PAIRS_PER_STEP, rpb.shape[2], LANES), lambda g, u: (layer, g, 0, 0))],
        out_specs=blk(),
        out_shape=jax.ShapeDtypeStruct((N_TOK, NA_W), BF16),
        scratch_shapes=[pltpu.VMEM((_PAIRS_PER_STEP, PAST, 2 * DH), BF16),
                        pltpu.VMEM((_PAIRS_PER_STEP, UNIT + PAST, 4 * DH), BF16),
                        pltpu.VMEM((2 * _PAIRS_PER_STEP, DEC_SEQ, DEC_SEQ), F32)],
        compiler_params=_cparams(("arbitrary", "arbitrary"), 40),
        name="attention",
    )(q, k, v, cache_k, cache_v, rpb_pad)


def _pool_kernel(u_ref, wp_ref, ps_ref, o_ref, *, layer):
    n = jnp.where(pl.program_id(0) < CTX_UNITS, SEQ, DEC_SEQ)
    t = lax.broadcasted_iota(jnp.int32, (UNIT, PG), 0)
    p = jnp.bitwise_and(t, n - 1)

    def fwd(a, s):
        return jnp.where(p < n - s, pltpu.roll(a, UNIT - s, 0), 0.0)

    def bwd(a, s):
        return jnp.where(p >= s, pltpu.roll(a, s, 0), 0.0)

    for gi, w in enumerate(POOL_WINDOWS):
        half = w // 2
        cols = slice(gi * PG, (gi + 1) * PG)
        g = u_ref[:, cols]
        f, b, s = g, g, 1
        while s < half:
            f = f + fwd(f, s)
            b = b + bwd(b, s)
            s *= 2
        tot = f + bwd(b, 1)
        cnt = (jnp.minimum(p + half, n) - jnp.maximum(p - half, 0)).astype(F32)
        dlt = tot / cnt - g
        out = _dot(dlt.astype(BF16), wp_ref[gi].astype(BF16)) * ps_ref[layer:layer + 1, cols]
        o_ref[:, cols] = out.astype(o_ref.dtype)


def _pool_mixer(u, w_pool, pool_scale, layer):
    return pl.pallas_call(
        functools.partial(_pool_kernel, layer=layer),
        grid=(N_UNITS,),
        in_specs=[pl.BlockSpec((UNIT, POOL_W), lambda i: (i, 0)),
                  pl.BlockSpec((None, len(POOL_WINDOWS), PG, PG), lambda i: (layer, 0, 0, 0)),
                  pl.BlockSpec((DEPTH, POOL_W), lambda i: (0, 0))],
        out_specs=pl.BlockSpec((UNIT, POOL_W), lambda i: (i, 0)),
        out_shape=jax.ShapeDtypeStruct((N_TOK, POOL_W), BF16),
        compiler_params=_cparams(("arbitrary",), 40),
        name="pool",
    )(u, w_pool, pool_scale)


_OUTPROJ_TILE = 512
_OUTPROJ_ROWS = 256


def _outproj_kernel(*refs, n_src, layer):
    x_refs = refs[:n_src]
    att_ref, pool_ref, w_ref, gate_ref, g2_ref, sh_ref, sc_ref, wr_ref, y_ref, h_ref, lg_ref = refs[n_src:]
    tm = _OUTPROJ_TILE
    for r in range(tm // _OUTPROJ_ROWS):
        rows = slice(r * _OUTPROJ_ROWS, (r + 1) * _OUTPROJ_ROWS)
        mix = _dot(att_ref[rows, :], w_ref[0:NA_W, :]) + _dot(pool_ref[rows, :], w_ref[NA_W:NA_W + POOL_W, :])
        y = _read_tokens(x_refs, tm, rows) + _mod_vec(gate_ref, tm) * mix
        y_ref[rows, :] = y
        h = _norm_mod(y, g2_ref[layer:layer + 1, :], _mod_vec(sh_ref, tm), _mod_vec(sc_ref, tm)).astype(BF16)
        h_ref[rows, :] = h
        lg_ref[rows, :] = _dot(h, wr_ref[...])


def _outproj(srcs, att, pool, mods, norm_gain, w_out_bf, w_router_pad, layer):
    tm = _OUTPROJ_TILE
    row = lambda width: pl.BlockSpec((tm, width), lambda i: (i, 0))
    return pl.pallas_call(
        functools.partial(_outproj_kernel, n_src=len(srcs), layer=layer),
        grid=(N_TOK // tm,),
        in_specs=_token_specs(srcs, tm, D) + [
            row(NA_W), row(POOL_W),
            _resident((D, D), lambda i: (0, 0)),
            _mod_spec(layer, 2), _gain_spec(), _mod_spec(layer, 3), _mod_spec(layer, 4),
            _resident((D, LANES), lambda i: (0, 0))],
        out_specs=[row(D), row(D), row(LANES)],
        out_shape=[jax.ShapeDtypeStruct((N_TOK, D), F32),
                   jax.ShapeDtypeStruct((N_TOK, D), BF16),
                   jax.ShapeDtypeStruct((N_TOK, LANES), F32)],
        compiler_params=_cparams(("arbitrary",), 52),
        name="outproj",
    )(*srcs, att, pool, w_out_bf, mods, norm_gain, mods, mods, w_router_pad)


_SEARCH_BITS = 3


def _choose_slots(aff, n_req, n_tok, cap):
    rows = jnp.concatenate([aff[:, r * n_tok:(r + 1) * n_tok] for r in range(n_req)], axis=0)
    n_rows = n_req * E
    thr = jnp.zeros((n_rows, 1), jnp.int32)
    hi = 31
    while hi > 0:
        lo = max(hi - _SEARCH_BITS, 0)
        n_cand = (1 << (hi - lo)) - 1
        hits = jnp.concatenate([(rows >= lax.bitcast_convert_type(thr | (j << lo), F32)).astype(F32)
                                for j in range(1, n_cand + 1)], axis=0)
        reached = (jnp.sum(hits, axis=-1, keepdims=True) >= cap).astype(jnp.int32)
        group = functools.reduce(jnp.add, [reached[j * n_rows:(j + 1) * n_rows] for j in range(n_cand)])
        thr = thr | lax.shift_left(group, jnp.int32(lo))
        hi = lo
    thr_val = lax.bitcast_convert_type(thr, F32)
    gt = rows > thr_val
    eq = rows == thr_val
    n_gt = jnp.sum(gt.astype(F32), axis=-1, keepdims=True)
    before = (lax.broadcasted_iota(jnp.int32, (n_tok, n_tok), 0)
              < lax.broadcasted_iota(jnp.int32, (n_tok, n_tok), 1)).astype(BF16)
    eq_rank = _dot(eq.astype(BF16), before)
    sel = gt | (eq & (eq_rank < cap - n_gt))
    slot = _dot(sel.astype(BF16), before)
    pieces = [jnp.where(sel[r * E:(r + 1) * E], slot[r * E:(r + 1) * E] + r * cap, -1.0) for r in range(n_req)]
    return jnp.concatenate(pieces, axis=1) if n_req > 1 else pieces[0]


def _route_kernel(lg_ref, pos_ref, aff_ref):
    lg = lg_ref[...].T[0:E, :]
    ex = jnp.exp(lg - jnp.max(lg, axis=0, keepdims=True))
    aff = ex / jnp.sum(ex, axis=0, keepdims=True)
    aff_ref[0] = aff

    @pl.when(pl.program_id(0) < CTX_UNITS)
    def _():
        pos_ref[0] = _choose_slots(aff, REQ_PER_CTX_UNIT, SEQ, CAP_CTX)

    @pl.when(pl.program_id(0) >= CTX_UNITS)
    def _():
        pos_ref[0] = _choose_slots(aff, 1, DEC_SEQ, CAP_LAT)


def _route(logits):
    out_blk = lambda: pl.BlockSpec((1, E, UNIT), lambda u: (u, 0, 0))
    return pl.pallas_call(
        _route_kernel,
        grid=(N_UNITS,),
        in_specs=[pl.BlockSpec((UNIT, LANES), lambda u: (u, 0))],
        out_specs=[out_blk(), out_blk()],
        out_shape=[jax.ShapeDtypeStruct((N_UNITS, E, UNIT), F32)] * 2,
        compiler_params=_cparams(("arbitrary",), 32),
        name="route",
    )(logits)


_COL_CHUNK = 512


def _onehot_rows(pos, p_ref, n_slots, first_slot):
    width = pos.shape[1]
    slot_id = (lax.broadcasted_iota(jnp.int32, (n_slots, width), 0) + first_slot).astype(F32)
    matches = []
    for e in range(E):
        match = pos[e:e + 1, :] == slot_id
        p_ref[e * n_slots:(e + 1) * n_slots, 0:width] = jnp.where(match, 1.0, 0.0).astype(BF16)
        matches.append(match)
    return matches


def _gather_kernel(pos_ref, aff_ref, h_ref, xs_ref, gs_ref, p_ref):
    def gather(tokens, slots, first_slot):
        n = slots.stop - slots.start
        aff = aff_ref[0, :, tokens]
        matches = _onehot_rows(pos_ref[0, :, tokens], p_ref, n, first_slot)
        for e in range(E):
            gate = jnp.sum(jnp.where(matches[e], aff[e:e + 1, :], 0.0), axis=-1, keepdims=True)
            gs_ref[e, slots, :] = jnp.broadcast_to(gate, (n, LANES))
        width = tokens.stop - tokens.start
        for c in range(D // _COL_CHUNK):
            cols = slice(c * _COL_CHUNK, (c + 1) * _COL_CHUNK)
            r = _dot(p_ref[0:E * n, 0:width], h_ref[tokens, cols])
            xs_ref[:, slots, cols] = r.reshape(E, n, _COL_CHUNK).astype(BF16)

    @pl.when(pl.program_id(0) < CTX_UNITS)
    def _():
        for r in range(REQ_PER_CTX_UNIT):
            gather(slice(r * SEQ, (r + 1) * SEQ), slice(r * CAP_CTX, (r + 1) * CAP_CTX), r * CAP_CTX)

    @pl.when(pl.program_id(0) >= CTX_UNITS)
    def _():
        gather(slice(0, UNIT), slice(0, SLOTS), 0)


def _gather(pos, aff, h):
    unit3 = lambda: pl.BlockSpec((1, E, UNIT), lambda u: (u, 0, 0))
    return pl.pallas_call(
        _gather_kernel,
        grid=(N_UNITS,),
        in_specs=[unit3(), unit3(), pl.BlockSpec((UNIT, D), lambda u: (u, 0))],
        out_specs=[pl.BlockSpec((E, SLOTS, D), lambda u: (0, u, 0)),
                   pl.BlockSpec((E, SLOTS, LANES), lambda u: (0, u, 0))],
        out_shape=[jax.ShapeDtypeStruct((E, N_SLOT, D), BF16),
                   jax.ShapeDtypeStruct((E, N_SLOT, LANES), F32)],
        scratch_shapes=[pltpu.VMEM((E * SLOTS, UNIT), BF16)],
        compiler_params=_cparams(("arbitrary",), 48),
        name="gather",
    )(pos, aff, h)


_FFN_ROWS = 256
_FFN_COLS = 256


def _ffn_up_kernel(xs_ref, wg_ref, wu_ref, o_ref):
    for c in range(wg_ref.shape[1] // _FFN_COLS):
        cols = slice(c * _FFN_COLS, (c + 1) * _FFN_COLS)
        wg = wg_ref[:, cols].astype(BF16)
        wu = wu_ref[:, cols].astype(BF16)
        for m in range(N_SLOT // _FFN_ROWS):
            rows = slice(m * _FFN_ROWS, (m + 1) * _FFN_ROWS)
            x = xs_ref[rows, :]
            o_ref[rows, cols] = (_silu(_dot(x, wg)) * _dot(x, wu)).astype(o_ref.dtype)


def _ffn_up(xs, w_gate, w_up, layer):
    tf = 512
    wspec = lambda: pl.BlockSpec((None, None, D, tf), lambda e, j: (layer, e, 0, j))
    return pl.pallas_call(
        _ffn_up_kernel,
        grid=(E, D_EXP // tf),
        in_specs=[pl.BlockSpec((None, N_SLOT, D), lambda e, j: (e, 0, 0)), wspec(), wspec()],
        out_specs=pl.BlockSpec((None, N_SLOT, tf), lambda e, j: (e, 0, j)),
        out_shape=jax.ShapeDtypeStruct((E, N_SLOT, D_EXP), BF16),
        compiler_params=_cparams(("arbitrary", "arbitrary"), 48),
        name="ffn_up",
    )(xs, w_gate, w_up)


def _ffn_down_kernel(h_ref, wd_ref, gs_ref, o_ref):
    for c in range(wd_ref.shape[1] // _FFN_COLS):
        cols = slice(c * _FFN_COLS, (c + 1) * _FFN_COLS)
        wd = wd_ref[:, cols].astype(BF16)
        for m in range(N_SLOT // _FFN_ROWS):
            rows = slice(m * _FFN_ROWS, (m + 1) * _FFN_ROWS)
            gate = jnp.concatenate([gs_ref[rows, :]] * (_FFN_COLS // LANES), axis=1)
            o_ref[rows, cols] = (_dot(h_ref[rows, :], wd) * gate).astype(o_ref.dtype)


def _ffn_down(hcur, w_down, gslot, layer):
    return pl.pallas_call(
        _ffn_down_kernel,
        grid=(E,),
        in_specs=[pl.BlockSpec((None, N_SLOT, D_EXP), lambda e: (e, 0, 0)),
                  pl.BlockSpec((None, None, D_EXP, D), lambda e: (layer, e, 0, 0)),
                  pl.BlockSpec((None, N_SLOT, LANES), lambda e: (e, 0, 0))],
        out_specs=pl.BlockSpec((None, N_SLOT, D), lambda e: (e, 0, 0)),
        out_shape=jax.ShapeDtypeStruct((E, N_SLOT, D), BF16),
        compiler_params=_cparams(("arbitrary",), 48),
        name="ffn_down",
    )(hcur, w_down, gslot)


_SC_TOK = 256


def _scatter_kernel(pos_ref, ys_ref, y_ref, gate_ref, *rest, final):
    if final:
        gain_ref, oc_ref, ol_ref, p_ref = rest
    else:
        o_ref, p_ref = rest

    def emit(updates, is_ctx):
        y = y_ref[...] + _mod_vec(gate_ref, UNIT) * jnp.concatenate(updates, axis=1)
        if not final:
            o_ref[...] = y
        else:
            r = y * lax.rsqrt(jnp.mean(y * y, axis=-1, keepdims=True) + RMS_EPS) * gain_ref[...]
            (oc_ref if is_ctx else ol_ref)[...] = r

    @pl.when(pl.program_id(0) < CTX_UNITS)
    def _():
        first = pl.multiple_of(pl.program_id(1) * CAP_CTX, CAP_CTX)
        _onehot_rows(pos_ref[0], p_ref, CAP_CTX, first)
        emit([_dot_tn(p_ref[0:E * CAP_CTX, :],
                      ys_ref[:, pl.ds(first, CAP_CTX), c * _COL_CHUNK:(c + 1) * _COL_CHUNK]
                      .reshape(E * CAP_CTX, _COL_CHUNK))
              for c in range(D // _COL_CHUNK)], True)

    @pl.when(pl.program_id(0) >= CTX_UNITS)
    def _():
        _onehot_rows(pos_ref[0], p_ref, SLOTS, 0)
        emit([_dot_tn(p_ref[...], ys_ref[:, :, c * _COL_CHUNK:(c + 1) * _COL_CHUNK].reshape(E * SLOTS, _COL_CHUNK))
              for c in range(D // _COL_CHUNK)], False)


def _scatter(pos, ys, y, mods, layer, final_gain=None):
    assert _SC_TOK == SEQ
    per_unit = UNIT // _SC_TOK
    n_ctx_tiles = N_CTX // _SC_TOK
    final = final_gain is not None
    tok = lambda: pl.BlockSpec((_SC_TOK, D), lambda u, s: (u * per_unit + s, 0))
    in_specs = [pl.BlockSpec((1, E, _SC_TOK), lambda u, s: (u, 0, s)),
                pl.BlockSpec((E, SLOTS, D), lambda u, s: (0, u, 0)),
                tok(),
                _mod_spec(layer, 5)]
    args = [pos, ys, y, mods]
    if final:
        in_specs.append(pl.BlockSpec((1, D), lambda u, s: (0, 0)))
        args.append(final_gain.reshape(1, D))
        out_specs = [pl.BlockSpec((_SC_TOK, D), lambda u, s: (jnp.minimum(u * per_unit + s, n_ctx_tiles - 1), 0)),
                     pl.BlockSpec((_SC_TOK, D), lambda u, s: (jnp.maximum(u * per_unit + s - n_ctx_tiles, 0), 0))]
        out_shape = [jax.ShapeDtypeStruct((N_CTX, D), F32), jax.ShapeDtypeStruct((N_LAT, D), F32)]
    else:
        out_specs = tok()
        out_shape = jax.ShapeDtypeStruct((N_TOK, D), F32)
    return pl.pallas_call(
        functools.partial(_scatter_kernel, final=final),
        grid=(N_UNITS, per_unit),
        in_specs=in_specs,
        out_specs=out_specs,
        out_shape=out_shape,
        scratch_shapes=[pltpu.VMEM((E * SLOTS, _SC_TOK), BF16)],
        compiler_params=_cparams(("arbitrary", "arbitrary"), 40),
        name="scatter",
    )(*args)


def kernel(x_prompt, x_sample, cache_k, cache_v, c, c_ctx, w_mod, b_mod, norm_mix, norm_ffn, w_in, rpb,
           w_pool, pool_scale, w_out, w_router, w_gate, w_up, w_down, norm_final):
    srcs = [x_prompt.reshape(N_CTX, D), x_sample.reshape(N_LAT, D)]
    cond = jnp.zeros((N_ROW, D), F32).at[0].set(c_ctx).at[1:1 + DEC_BATCH].set(c)
    mods = _adaln(cond, w_mod, b_mod)
    ck = cache_k.reshape(DEC_BATCH, DEPTH, PAST, NA_W)
    cv = cache_v.reshape(DEC_BATCH, DEPTH, PAST, NA_W)

    new_kv = []
    for l in range(DEPTH):
        q, k, v, u, *new_kv = _inproj(srcs, mods, norm_mix, _to_bf16(w_in, l), l, new_kv)
        att = _attention(q, k, v, ck, cv, rpb, l)
        pool = _pool_mixer(u, w_pool, pool_scale, l)

        w_router_pad = jnp.pad(w_router[l], ((0, 0), (0, LANES - E))).astype(BF16)
        y, h, logits = _outproj(srcs, att, pool, mods, norm_ffn, _to_bf16(w_out, l), w_router_pad, l)

        pos, aff = _route(logits)
        xs, gslot = _gather(pos, aff, h)
        ys = _ffn_down(_ffn_up(xs, w_gate, w_up, l), w_down, gslot, l)
        srcs = _scatter(pos, ys, y, mods, l, norm_final if l == DEPTH - 1 else None)
        srcs = list(srcs) if l == DEPTH - 1 else [srcs]

    y_prompt, y_sample = srcs
    new_k, new_v = new_kv
    return (y_prompt.reshape(BATCH, SEQ, D), y_sample.reshape(DEC_BATCH, DEC_SEQ, D),
            new_k.reshape(BATCH, DEPTH, SEQ, H, DH), new_v.reshape(BATCH, DEPTH, SEQ, H, DH))
```

```python
import functools

import jax
import jax.numpy as jnp
from jax import lax
from jax.experimental import pallas as pl
from jax.experimental.pallas import tpu as pltpu

F32 = jnp.float32
BF16 = jnp.bfloat16

D = 2048
BATCH, SEQ = 16, 256
DEPTH = 2
DEC_BATCH, DEC_SEQ = 8, 1024
PAST = 512
GRID_W = 64
GRID_H = DEC_SEQ // GRID_W
H, DH = 16, 64
NA_W = H * DH
POOL_WINDOWS = (2, 4, 8, 16)
PG = 256
POOL_W = 1024
IN_W = 3 * NA_W + POOL_W
NA_ROWS, NA_COLS = 8, 16
E = 16
D_EXP = 1024
N_MOD = 6
RMS_EPS = 1e-6
NEG_INF = -1e30

N_CTX = BATCH * SEQ
N_LAT = DEC_BATCH * DEC_SEQ
N_TOK = N_CTX + N_LAT
UNIT = 1024
N_UNITS = N_TOK // UNIT
CTX_UNITS = N_CTX // UNIT
REQ_PER_CTX_UNIT = UNIT // SEQ
SLOTS = 128
CAP_CTX = 2 * SEQ // E
CAP_LAT = 2 * DEC_SEQ // E
N_SLOT = N_UNITS * SLOTS
N_ROW = 16
LANES = 128
MIB = 1024 * 1024


def _cparams(sem, vmem_mib):
    return pltpu.CompilerParams(dimension_semantics=sem, vmem_limit_bytes=vmem_mib * MIB)


def _resident(block_shape, index_map):
    return pl.BlockSpec(block_shape, index_map, pipeline_mode=pl.Buffered(1))


def _dot(a, b):
    return jnp.dot(a, b, preferred_element_type=F32)


def _dot_nt(a, b):
    return lax.dot_general(a, b, (((1,), (1,)), ((), ())), preferred_element_type=F32)


def _dot_tn(a, b):
    return lax.dot_general(a, b, (((0,), (0,)), ((), ())), preferred_element_type=F32)


def _silu(x):
    return x / (1.0 + jnp.exp(-x))


def _norm_mod(x, gain, shift, scale):
    y = x * lax.rsqrt(jnp.mean(x * x, axis=-1, keepdims=True) + RMS_EPS)
    return (y * gain) * (1.0 + scale) + shift


def _mod_row_of_tile(i, tile):
    per_req = DEC_SEQ // tile
    n_ctx_tiles = N_CTX // tile
    return jnp.where(i < n_ctx_tiles, 0, 1 + (i - n_ctx_tiles) // per_req)


def _mod_spec(layer, which):
    return pl.BlockSpec((None, N_ROW, D), lambda *ids: (layer, 0, which))


def _mod_vec(ref, tile, axis=0):
    return ref[pl.ds(_mod_row_of_tile(pl.program_id(axis), tile), 1), :]


def _gain_spec():
    return pl.BlockSpec((DEPTH, D), lambda *ids: (0, 0))


def _token_specs(srcs, tile, width):
    if len(srcs) == 1:
        return [pl.BlockSpec((tile, width), lambda i: (i, 0))]
    n_ctx_tiles = N_CTX // tile
    return [pl.BlockSpec((tile, width), lambda i: (jnp.minimum(i, n_ctx_tiles - 1), 0)),
            pl.BlockSpec((tile, width), lambda i: (jnp.maximum(i - n_ctx_tiles, 0), 0))]


def _read_tokens(refs, tile, rows=slice(None)):
    if len(refs) == 1:
        return refs[0][rows, :]
    return jnp.where(pl.program_id(0) < N_CTX // tile, refs[0][rows, :], refs[1][rows, :])


def _adaln_kernel(c_ref, w_ref, b_ref, o_ref):
    a = _silu(c_ref[...]).astype(BF16)
    o_ref[0] = _dot(a, w_ref[0].astype(BF16)) + b_ref[pl.ds(pl.program_id(0), 1), :]


def _adaln(cond, w_mod, b_mod):
    tn = 1024
    return pl.pallas_call(
        _adaln_kernel,
        grid=(DEPTH, N_MOD * D // tn),
        in_specs=[pl.BlockSpec((N_ROW, D), lambda l, j: (0, 0)),
                  pl.BlockSpec((1, D, tn), lambda l, j: (l, 0, j)),
                  pl.BlockSpec((DEPTH, tn), lambda l, j: (0, j))],
        out_specs=pl.BlockSpec((1, N_ROW, tn), lambda l, j: (l, 0, j)),
        out_shape=jax.ShapeDtypeStruct((DEPTH, N_ROW, N_MOD * D), F32),
        compiler_params=_cparams(("arbitrary", "arbitrary"), 40),
        name="adaln",
    )(cond, w_mod, b_mod)


def _cast_kernel(w_ref, o_ref):
    o_ref[...] = w_ref[...].astype(BF16)


def _to_bf16(w, layer):
    rows, cols = w.shape[1:]
    tr = 256
    return pl.pallas_call(
        _cast_kernel,
        grid=(rows // tr,),
        in_specs=[pl.BlockSpec((None, tr, cols), lambda i: (layer, i, 0))],
        out_specs=pl.BlockSpec((tr, cols), lambda i: (i, 0)),
        out_shape=jax.ShapeDtypeStruct((rows, cols), BF16),
        compiler_params=_cparams(("arbitrary",), 32),
        name="cast_bf16",
    )(w)


_INPROJ_TILE = 256


def _inproj_kernel(*refs, n_src, n_prev, layer, tm):
    x_refs = refs[:n_src]
    g_ref, sh_ref, sc_ref, w_ref = refs[n_src:n_src + 4]
    q_ref, k_ref, v_ref, u_ref, kf_ref, vf_ref = refs[n_src + 4 + n_prev:]
    for r in range(tm // SEQ):
        rows = slice(r * SEQ, (r + 1) * SEQ)
        h = _norm_mod(_read_tokens(x_refs, tm, rows), g_ref[layer:layer + 1, :], _mod_vec(sh_ref, tm),
                      _mod_vec(sc_ref, tm)).astype(BF16)
        q_ref[rows, :] = (_dot(h, w_ref[:, 0:NA_W]) * (DH ** -0.5)).astype(BF16)
        k = _dot(h, w_ref[:, NA_W:2 * NA_W])
        v = _dot(h, w_ref[:, 2 * NA_W:3 * NA_W])
        k_ref[rows, :] = k.astype(BF16)
        v_ref[rows, :] = v.astype(BF16)
        u_ref[rows, :] = _dot(h, w_ref[:, 3 * NA_W:IN_W])

        @pl.when(pl.program_id(0) < N_CTX // tm)
        def _():
            for slab in range(kf_ref.shape[1]):
                mine = slab == (layer if n_prev == 0 else 0)
                kf_ref[r, slab] = k if mine else jnp.zeros_like(k)
                vf_ref[r, slab] = v if mine else jnp.zeros_like(v)


def _inproj(srcs, mods, norm_gain, w_in_bf, layer, kv_prev):
    tm = _INPROJ_TILE if len(srcs) > 1 else 2 * _INPROJ_TILE
    n_ctx_tiles = N_CTX // tm
    tok = lambda: pl.BlockSpec((tm, NA_W), lambda i: (i, 0))
    if kv_prev:
        ctx_only = lambda: pl.BlockSpec((tm // SEQ, 1, SEQ, NA_W),
                                        lambda i: (jnp.minimum(i, n_ctx_tiles - 1), layer, 0, 0))
    else:
        ctx_only = lambda: pl.BlockSpec((tm // SEQ, DEPTH, SEQ, NA_W),
                                        lambda i: (jnp.minimum(i, n_ctx_tiles - 1), 0, 0, 0))
    n_in = len(srcs) + 4
    return pl.pallas_call(
        functools.partial(_inproj_kernel, n_src=len(srcs), n_prev=len(kv_prev), layer=layer, tm=tm),
        grid=(N_TOK // tm,),
        in_specs=_token_specs(srcs, tm, D) + [
            _gain_spec(), _mod_spec(layer, 0), _mod_spec(layer, 1),
            _resident((D, IN_W), lambda i: (0, 0))] + [pl.BlockSpec(memory_space=pl.ANY)] * len(kv_prev),
        out_specs=[tok(), tok(), tok(), tok(), ctx_only(), ctx_only()],
        out_shape=[jax.ShapeDtypeStruct((N_TOK, NA_W), BF16)] * 3
        + [jax.ShapeDtypeStruct((N_TOK, POOL_W), F32)]
        + [jax.ShapeDtypeStruct((BATCH, DEPTH, SEQ, NA_W), F32)] * 2,
        input_output_aliases={n_in + j: 4 + j for j in range(len(kv_prev))},
        compiler_params=_cparams(("arbitrary",), 56),
        name="inproj",
    )(*srcs, norm_gain, mods, mods, w_in_bf, *kv_prev)


def _na_window_start(qr):
    return min(max(qr - NA_ROWS // 2, 0), GRID_H - NA_ROWS)


def _fill_na_bias(rpb_ref, o_ref):
    assert LANES == 2 * GRID_W
    qc = lax.broadcasted_iota(jnp.int32, (GRID_W, LANES), 0)
    lane = lax.broadcasted_iota(jnp.int32, (GRID_W, LANES), 1)
    kc = jnp.bitwise_and(lane, GRID_W - 1)
    cs = jnp.clip(qc - NA_COLS // 2, 0, GRID_W - NA_COLS)
    in_col = (kc >= cs) & (kc < cs + NA_COLS)
    keep_lanes = {(True, True): in_col, (True, False): in_col & (lane < GRID_W),
                  (False, True): in_col & (lane >= GRID_W)}
    n_dr = 2 * NA_ROWS - 1
    zero_rows = jnp.zeros((GRID_W, LANES), F32)

    def rpb_rows(dr):
        if not 0 <= dr < n_dr:
            return zero_rows
        return jnp.broadcast_to(rpb_ref[dr:dr + 1, :], (GRID_W, LANES))

    pair_cache = {}

    def pair_tile(dr, use_first, use_second):
        if not (use_first or use_second):
            return jnp.full((GRID_W, LANES), NEG_INF, F32)
        key = (dr, use_first, use_second)
        if key not in pair_cache:
            src = rpb_rows(dr) + pltpu.roll(rpb_rows(dr + 1), GRID_W, 1)
            toep = pltpu.roll(src, LANES - (NA_COLS - 1), 1, stride=1, stride_axis=0)
            pair_cache[key] = jnp.where(keep_lanes[(use_first, use_second)], toep, NEG_INF)
        return pair_cache[key]

    for qr in range(GRID_H):
        rs = _na_window_start(qr)
        for kr in range(0, GRID_H, 2):
            use = [rs <= k < rs + NA_ROWS for k in (kr, kr + 1)]
            o_ref[qr * GRID_W:(qr + 1) * GRID_W, kr * GRID_W:(kr + 2) * GRID_W] = pair_tile(
                kr - qr + NA_ROWS - 1, use[0], use[1])


_NA_Q_BLOCK = 4 * GRID_W


def _na_key_range(qb):
    rows_per_block = _NA_Q_BLOCK // GRID_W
    lo = _na_window_start(qb * rows_per_block) * GRID_W
    hi = (_na_window_start((qb + 1) * rows_per_block - 1) + NA_ROWS) * GRID_W
    return (lo // _NA_Q_BLOCK * _NA_Q_BLOCK, -(-hi // _NA_Q_BLOCK) * _NA_Q_BLOCK)


_NA_KEY_RANGES = [_na_key_range(qb) for qb in range(DEC_SEQ // _NA_Q_BLOCK)]


_PAIRS_PER_STEP = 4


def _attn_kernel(q_ref, k_ref, v_ref, ck_ref, cv_ref, rpb_ref, o_ref, kc_ref, vv_ref, tab_ref):
    @pl.when(pl.program_id(1) == CTX_UNITS)
    def _():
        for h in range(2 * _PAIRS_PER_STEP):
            _fill_na_bias(rpb_ref.at[h], tab_ref.at[h])

    lane = lax.broadcasted_iota(jnp.int32, (1, 2 * DH), 1)
    head_lanes = [lane < DH, lane >= DH]

    def probs(scores):
        m = functools.reduce(jnp.maximum, [jnp.max(s, axis=-1, keepdims=True) for s in scores])
        return [jnp.exp(s - m).astype(BF16) for s in scores]

    def merge_heads(acc):
        outs = [a[:, 0:2 * DH] / a[:, 2 * DH:4 * DH] for a in acc]
        return jnp.where(head_lanes[0], outs[0], outs[1]).astype(o_ref.dtype)

    pairs = [(pp, slice(pp * 2 * DH, (pp + 1) * 2 * DH)) for pp in range(_PAIRS_PER_STEP)]
    for pp, pair in pairs:
        vv_ref[pp, 0:UNIT, 0:2 * DH] = v_ref[:, pair]
        vv_ref[pp, :, 2 * DH:4 * DH] = jnp.ones((UNIT + PAST, 2 * DH), BF16)

    def head_queries(rows, pair):
        q = q_ref[rows, pair]
        return [jnp.where(m, q, jnp.zeros_like(q)) for m in head_lanes]

    @pl.when(pl.program_id(1) < CTX_UNITS)
    def _():
        for pp, pair in pairs:
            for r in range(REQ_PER_CTX_UNIT):
                rows = slice(r * SEQ, (r + 1) * SEQ)
                acc = []
                for qh in head_queries(rows, pair):
                    (p,) = probs([_dot_nt(qh, k_ref[rows, pair])])
                    acc.append(_dot(p, vv_ref[pp, rows, :]))
                o_ref[rows, pair] = merge_heads(acc)

    @pl.when(pl.program_id(1) >= CTX_UNITS)
    def _():
        for pp, pair in pairs:
            kc_ref[pp] = ck_ref[0, 0, :, pair].astype(BF16)
            vv_ref[pp, UNIT:UNIT + PAST, 0:2 * DH] = cv_ref[0, 0, :, pair].astype(BF16)
            for qb, (k_lo, k_hi) in enumerate(_NA_KEY_RANGES):
                rows = slice(qb * _NA_Q_BLOCK, (qb + 1) * _NA_Q_BLOCK)
                keys = slice(k_lo, k_hi)
                acc = []
                for hh, qh in enumerate(head_queries(rows, pair)):
                    s_lat = _dot_nt(qh, k_ref[keys, pair]) + tab_ref[2 * pp + hh, rows, keys]
                    s_ctx = _dot_nt(qh, kc_ref[pp])
                    p_lat, p_ctx = probs([s_lat, s_ctx])
                    acc.append(_dot(p_lat, vv_ref[pp, keys, :]) + _dot(p_ctx, vv_ref[pp, UNIT:UNIT + PAST, :]))
                o_ref[rows, pair] = merge_heads(acc)


def _attention(q, k, v, cache_k, cache_v, rpb, layer):
    width = _PAIRS_PER_STEP * 2 * DH
    blk = lambda: pl.BlockSpec((UNIT, width), lambda g, u: (u, g))
    cache = lambda: pl.BlockSpec((1, 1, PAST, width), lambda g, u: (jnp.maximum(u - CTX_UNITS, 0), layer, 0, g))
    rpb_pad = jnp.pad(rpb, ((0, 0), (0, 0), (0, 0), (0, LANES - rpb.shape[-1])))
    return pl.pallas_call(
        _attn_kernel,
        grid=(NA_W // width, N_UNITS),
        in_specs=[blk(), blk(), blk(), cache(), cache(),
                  pl.BlockSpec((None, 2 * _PAIRS_PER_STEP, rpb.shape[2], LANES), lambda g, u: (layer, g, 0, 0))],
        out_specs=blk(),
        out_shape=jax.ShapeDtypeStruct((N_TOK, NA_W), BF16),
        scratch_shapes=[pltpu.VMEM((_PAIRS_PER_STEP, PAST, 2 * DH), BF16),
                        pltpu.VMEM((_PAIRS_PER_STEP, UNIT + PAST, 4 * DH), BF16),
                        pltpu.VMEM((2 * _PAIRS_PER_STEP, DEC_SEQ, DEC_SEQ), F32)],
        compiler_params=_cparams(("arbitrary", "arbitrary"), 56),
        name="attention",
    )(q, k, v, cache_k, cache_v, rpb_pad)


_POOL_EDGE = 8


def _pool_kernel(u_ref, wp_ref, ps_ref, o_ref, *, layer):
    n = jnp.where(pl.program_id(0) < CTX_UNITS, SEQ, DEC_SEQ)
    assert max(POOL_WINDOWS) // 2 <= _POOL_EDGE
    blocks = range(UNIT // SEQ)

    def edge_pos(first_row):
        t = lax.broadcasted_iota(jnp.int32, (_POOL_EDGE, PG), 0) + first_row
        return jnp.bitwise_and(t, n - 1)

    head_pos = [edge_pos(k * SEQ) for k in blocks]
    tail_pos = [edge_pos((k + 1) * SEQ - _POOL_EDGE) for k in blocks]

    def patch(x, head_fn=None, tail_fn=None, body_fn=None):
        pieces = []
        for k in blocks:
            lo, hi = k * SEQ, (k + 1) * SEQ
            head, body, tail = x[lo:lo + _POOL_EDGE], x[lo + _POOL_EDGE:hi - _POOL_EDGE], x[hi - _POOL_EDGE:hi]
            pieces += [head_fn(head, k) if head_fn else head, body_fn(body) if body_fn else body,
                       tail_fn(tail, k) if tail_fn else tail]
        return jnp.concatenate(pieces, axis=0)

    def fwd(a, s):
        return patch(pltpu.roll(a, UNIT - s, 0), tail_fn=lambda x, k: jnp.where(tail_pos[k] < n - s, x, 0.0))

    def bwd(a, s):
        return patch(pltpu.roll(a, s, 0), head_fn=lambda x, k: jnp.where(head_pos[k] >= s, x, 0.0))

    for gi, w in enumerate(POOL_WINDOWS):
        half = w // 2
        cols = slice(gi * PG, (gi + 1) * PG)
        g = u_ref[:, cols]
        f, b, s = g, g, 1
        while s < half:
            f = f + fwd(f, s)
            b = b + bwd(b, s)
            s *= 2
        tot = f + bwd(b, 1)

        def edge_mean(x, pos):
            cnt = jnp.minimum(pos + half, n) - jnp.maximum(pos - half, 0)
            return x / cnt.astype(F32)

        mean = patch(tot, head_fn=lambda x, k: edge_mean(x, head_pos[k]),
                     tail_fn=lambda x, k: edge_mean(x, tail_pos[k]), body_fn=lambda x: x * (1.0 / w))
        dlt = mean - g
        out = _dot(dlt.astype(BF16), wp_ref[gi].astype(BF16)) * ps_ref[layer:layer + 1, cols]
        o_ref[:, cols] = out.astype(o_ref.dtype)


def _pool_mixer(u, w_pool, pool_scale, layer):
    return pl.pallas_call(
        functools.partial(_pool_kernel, layer=layer),
        grid=(N_UNITS,),
        in_specs=[pl.BlockSpec((UNIT, POOL_W), lambda i: (i, 0)),
                  pl.BlockSpec((None, len(POOL_WINDOWS), PG, PG), lambda i: (layer, 0, 0, 0)),
                  pl.BlockSpec((DEPTH, POOL_W), lambda i: (0, 0))],
        out_specs=pl.BlockSpec((UNIT, POOL_W), lambda i: (i, 0)),
        out_shape=jax.ShapeDtypeStruct((N_TOK, POOL_W), BF16),
        compiler_params=_cparams(("arbitrary",), 40),
        name="pool",
    )(u, w_pool, pool_scale)


_OUTPROJ_TILE = 512
_OUTPROJ_ROWS = 256


def _outproj_kernel(*refs, n_src, layer):
    x_refs = refs[:n_src]
    att_ref, pool_ref, w_ref, gate_ref, g2_ref, sh_ref, sc_ref, wr_ref, y_ref, h_ref, lg_ref = refs[n_src:]
    tm = _OUTPROJ_TILE
    for r in range(tm // _OUTPROJ_ROWS):
        rows = slice(r * _OUTPROJ_ROWS, (r + 1) * _OUTPROJ_ROWS)
        mix = _dot(att_ref[rows, :], w_ref[0:NA_W, :]) + _dot(pool_ref[rows, :], w_ref[NA_W:NA_W + POOL_W, :])
        y = _read_tokens(x_refs, tm, rows) + _mod_vec(gate_ref, tm) * mix
        y_ref[rows, :] = y
        h = _norm_mod(y, g2_ref[layer:layer + 1, :], _mod_vec(sh_ref, tm), _mod_vec(sc_ref, tm)).astype(BF16)
        h_ref[rows, :] = h
        lg_ref[rows, :] = _dot(h, wr_ref[...])


def _outproj(srcs, att, pool, mods, norm_gain, w_out_bf, w_router_pad, layer):
    tm = _OUTPROJ_TILE
    row = lambda width: pl.BlockSpec((tm, width), lambda i: (i, 0))
    return pl.pallas_call(
        functools.partial(_outproj_kernel, n_src=len(srcs), layer=layer),
        grid=(N_TOK // tm,),
        in_specs=_token_specs(srcs, tm, D) + [
            row(NA_W), row(POOL_W),
            _resident((D, D), lambda i: (0, 0)),
            _mod_spec(layer, 2), _gain_spec(), _mod_spec(layer, 3), _mod_spec(layer, 4),
            _resident((D, LANES), lambda i: (0, 0))],
        out_specs=[row(D), row(D), row(LANES)],
        out_shape=[jax.ShapeDtypeStruct((N_TOK, D), F32),
                   jax.ShapeDtypeStruct((N_TOK, D), BF16),
                   jax.ShapeDtypeStruct((N_TOK, LANES), F32)],
        compiler_params=_cparams(("arbitrary",), 52),
        name="outproj",
    )(*srcs, att, pool, w_out_bf, mods, norm_gain, mods, mods, w_router_pad)


_SEARCH_BITS = 3


def _choose_slots(aff, n_req, n_tok, cap):
    rows = jnp.concatenate([aff[:, r * n_tok:(r + 1) * n_tok] for r in range(n_req)], axis=0)
    n_rows = n_req * E
    thr = jnp.zeros((n_rows, 1), jnp.int32)
    hi = 31
    while hi > 0:
        lo = max(hi - _SEARCH_BITS, 0)
        n_cand = (1 << (hi - lo)) - 1
        hits = jnp.concatenate([(rows >= lax.bitcast_convert_type(thr | (j << lo), F32)).astype(F32)
                                for j in range(1, n_cand + 1)], axis=0)
        reached = (jnp.sum(hits, axis=-1, keepdims=True) >= cap).astype(jnp.int32)
        group = functools.reduce(jnp.add, [reached[j * n_rows:(j + 1) * n_rows] for j in range(n_cand)])
        thr = thr | lax.shift_left(group, jnp.int32(lo))
        hi = lo
    thr_val = lax.bitcast_convert_type(thr, F32)
    gt = rows > thr_val
    eq = rows == thr_val
    n_gt = jnp.sum(gt.astype(F32), axis=-1, keepdims=True)
    before = (lax.broadcasted_iota(jnp.int32, (n_tok, n_tok), 0)
              < lax.broadcasted_iota(jnp.int32, (n_tok, n_tok), 1)).astype(BF16)
    eq_rank = _dot(eq.astype(BF16), before)
    sel = gt | (eq & (eq_rank < cap - n_gt))
    slot = _dot(sel.astype(BF16), before)
    pieces = [jnp.where(sel[r * E:(r + 1) * E], slot[r * E:(r + 1) * E] + r * cap, -1.0) for r in range(n_req)]
    return jnp.concatenate(pieces, axis=1) if n_req > 1 else pieces[0]


def _route_kernel(lg_ref, pos_ref, aff_ref):
    lg = lg_ref[...].T[0:E, :]
    ex = jnp.exp(lg - jnp.max(lg, axis=0, keepdims=True))
    aff = ex / jnp.sum(ex, axis=0, keepdims=True)
    aff_ref[0] = aff

    @pl.when(pl.program_id(0) < CTX_UNITS)
    def _():
        pos_ref[0] = _choose_slots(aff, REQ_PER_CTX_UNIT, SEQ, CAP_CTX)

    @pl.when(pl.program_id(0) >= CTX_UNITS)
    def _():
        pos_ref[0] = _choose_slots(aff, 1, DEC_SEQ, CAP_LAT)


def _route(logits):
    out_blk = lambda: pl.BlockSpec((1, E, UNIT), lambda u: (u, 0, 0))
    return pl.pallas_call(
        _route_kernel,
        grid=(N_UNITS,),
        in_specs=[pl.BlockSpec((UNIT, LANES), lambda u: (u, 0))],
        out_specs=[out_blk(), out_blk()],
        out_shape=[jax.ShapeDtypeStruct((N_UNITS, E, UNIT), F32)] * 2,
        compiler_params=_cparams(("arbitrary",), 32),
        name="route",
    )(logits)


_COL_CHUNK = 512


def _onehot_rows(pos, p_ref, n_slots, first_slot):
    width = pos.shape[1]
    slot_id = (lax.broadcasted_iota(jnp.int32, (n_slots, width), 0) + first_slot).astype(F32)
    matches = []
    for e in range(E):
        match = pos[e:e + 1, :] == slot_id
        p_ref[e * n_slots:(e + 1) * n_slots, 0:width] = jnp.where(match, 1.0, 0.0).astype(BF16)
        matches.append(match)
    return matches


def _gather_kernel(pos_ref, aff_ref, h_ref, xs_ref, gs_ref, p_ref):
    def gather(tokens, slots, first_slot):
        n = slots.stop - slots.start
        aff = aff_ref[0, :, tokens]
        matches = _onehot_rows(pos_ref[0, :, tokens], p_ref, n, first_slot)
        for e in range(E):
            gate = jnp.sum(jnp.where(matches[e], aff[e:e + 1, :], 0.0), axis=-1, keepdims=True)
            gs_ref[e, slots, :] = jnp.broadcast_to(gate, (n, LANES))
        width = tokens.stop - tokens.start
        for c in range(D // _COL_CHUNK):
            cols = slice(c * _COL_CHUNK, (c + 1) * _COL_CHUNK)
            r = _dot(p_ref[0:E * n, 0:width], h_ref[tokens, cols])
            xs_ref[:, slots, cols] = r.reshape(E, n, _COL_CHUNK).astype(BF16)

    @pl.when(pl.program_id(0) < CTX_UNITS)
    def _():
        for r in range(REQ_PER_CTX_UNIT):
            gather(slice(r * SEQ, (r + 1) * SEQ), slice(r * CAP_CTX, (r + 1) * CAP_CTX), r * CAP_CTX)

    @pl.when(pl.program_id(0) >= CTX_UNITS)
    def _():
        gather(slice(0, UNIT), slice(0, SLOTS), 0)


def _gather(pos, aff, h):
    unit3 = lambda: pl.BlockSpec((1, E, UNIT), lambda u: (u, 0, 0))
    return pl.pallas_call(
        _gather_kernel,
        grid=(N_UNITS,),
        in_specs=[unit3(), unit3(), pl.BlockSpec((UNIT, D), lambda u: (u, 0))],
        out_specs=[pl.BlockSpec((E, SLOTS, D), lambda u: (0, u, 0)),
                   pl.BlockSpec((E, SLOTS, LANES), lambda u: (0, u, 0))],
        out_shape=[jax.ShapeDtypeStruct((E, N_SLOT, D), BF16),
                   jax.ShapeDtypeStruct((E, N_SLOT, LANES), F32)],
        scratch_shapes=[pltpu.VMEM((E * SLOTS, UNIT), BF16)],
        compiler_params=_cparams(("arbitrary",), 48),
        name="gather",
    )(pos, aff, h)


_FFN_ROWS = 256
_FFN_COLS = 256


def _ffn_up_kernel(xs_ref, wg_ref, wu_ref, o_ref):
    for c in range(wg_ref.shape[1] // _FFN_COLS):
        cols = slice(c * _FFN_COLS, (c + 1) * _FFN_COLS)
        wg = wg_ref[:, cols].astype(BF16)
        wu = wu_ref[:, cols].astype(BF16)
        for m in range(N_SLOT // _FFN_ROWS):
            rows = slice(m * _FFN_ROWS, (m + 1) * _FFN_ROWS)
            x = xs_ref[rows, :]
            o_ref[rows, cols] = (_silu(_dot(x, wg)) * _dot(x, wu)).astype(o_ref.dtype)


def _ffn_up(xs, w_gate, w_up, layer):
    tf = 512
    wspec = lambda: pl.BlockSpec((None, None, D, tf), lambda e, j: (layer, e, 0, j))
    return pl.pallas_call(
        _ffn_up_kernel,
        grid=(E, D_EXP // tf),
        in_specs=[pl.BlockSpec((None, N_SLOT, D), lambda e, j: (e, 0, 0)), wspec(), wspec()],
        out_specs=pl.BlockSpec((None, N_SLOT, tf), lambda e, j: (e, 0, j)),
        out_shape=jax.ShapeDtypeStruct((E, N_SLOT, D_EXP), BF16),
        compiler_params=_cparams(("arbitrary", "arbitrary"), 48),
        name="ffn_up",
    )(xs, w_gate, w_up)


def _ffn_down_kernel(h_ref, wd_ref, gs_ref, o_ref):
    for c in range(wd_ref.shape[1] // _FFN_COLS):
        cols = slice(c * _FFN_COLS, (c + 1) * _FFN_COLS)
        wd = wd_ref[:, cols].astype(BF16)
        for m in range(N_SLOT // _FFN_ROWS):
            rows = slice(m * _FFN_ROWS, (m + 1) * _FFN_ROWS)
            gate = jnp.concatenate([gs_ref[rows, :]] * (_FFN_COLS // LANES), axis=1)
            o_ref[rows, cols] = (_dot(h_ref[rows, :], wd) * gate).astype(o_ref.dtype)


def _ffn_down(hcur, w_down, gslot, layer):
    return pl.pallas_call(
        _ffn_down_kernel,
        grid=(E,),
        in_specs=[pl.BlockSpec((None, N_SLOT, D_EXP), lambda e: (e, 0, 0)),
                  pl.BlockSpec((None, None, D_EXP, D), lambda e: (layer, e, 0, 0)),
                  pl.BlockSpec((None, N_SLOT, LANES), lambda e: (e, 0, 0))],
        out_specs=pl.BlockSpec((None, N_SLOT, D), lambda e: (e, 0, 0)),
        out_shape=jax.ShapeDtypeStruct((E, N_SLOT, D), BF16),
        compiler_params=_cparams(("arbitrary",), 48),
        name="ffn_down",
    )(hcur, w_down, gslot)


_SC_TOK = 256


def _scatter_kernel(pos_ref, ys_ref, y_ref, gate_ref, *rest, final):
    if final:
        gain_ref, oc_ref, ol_ref, p_ref = rest
    else:
        o_ref, p_ref = rest

    def emit(updates, is_ctx):
        y = y_ref[...] + _mod_vec(gate_ref, UNIT) * jnp.concatenate(updates, axis=1)
        if not final:
            o_ref[...] = y
        else:
            r = y * lax.rsqrt(jnp.mean(y * y, axis=-1, keepdims=True) + RMS_EPS) * gain_ref[...]
            (oc_ref if is_ctx else ol_ref)[...] = r

    @pl.when(pl.program_id(0) < CTX_UNITS)
    def _():
        first = pl.multiple_of(pl.program_id(1) * CAP_CTX, CAP_CTX)
        _onehot_rows(pos_ref[0], p_ref, CAP_CTX, first)
        emit([_dot_tn(p_ref[0:E * CAP_CTX, :],
                      ys_ref[:, pl.ds(first, CAP_CTX), c * _COL_CHUNK:(c + 1) * _COL_CHUNK]
                      .reshape(E * CAP_CTX, _COL_CHUNK))
              for c in range(D // _COL_CHUNK)], True)

    @pl.when(pl.program_id(0) >= CTX_UNITS)
    def _():
        _onehot_rows(pos_ref[0], p_ref, SLOTS, 0)
        emit([_dot_tn(p_ref[...], ys_ref[:, :, c * _COL_CHUNK:(c + 1) * _COL_CHUNK].reshape(E * SLOTS, _COL_CHUNK))
              for c in range(D // _COL_CHUNK)], False)


def _scatter(pos, ys, y, mods, layer, final_gain=None):
    assert _SC_TOK == SEQ
    per_unit = UNIT // _SC_TOK
    n_ctx_tiles = N_CTX // _SC_TOK
    final = final_gain is not None
    tok = lambda: pl.BlockSpec((_SC_TOK, D), lambda u, s: (u * per_unit + s, 0))
    in_specs = [pl.BlockSpec((1, E, _SC_TOK), lambda u, s: (u, 0, s)),
                pl.BlockSpec((E, SLOTS, D), lambda u, s: (0, u, 0)),
                tok(),
                _mod_spec(layer, 5)]
    args = [pos, ys, y, mods]
    if final:
        in_specs.append(pl.BlockSpec((1, D), lambda u, s: (0, 0)))
        args.append(final_gain.reshape(1, D))
        out_specs = [pl.BlockSpec((_SC_TOK, D), lambda u, s: (jnp.minimum(u * per_unit + s, n_ctx_tiles - 1), 0)),
                     pl.BlockSpec((_SC_TOK, D), lambda u, s: (jnp.maximum(u * per_unit + s - n_ctx_tiles, 0), 0))]
        out_shape = [jax.ShapeDtypeStruct((N_CTX, D), F32), jax.ShapeDtypeStruct((N_LAT, D), F32)]
    else:
        out_specs = tok()
        out_shape = jax.ShapeDtypeStruct((N_TOK, D), F32)
    return pl.pallas_call(
        functools.partial(_scatter_kernel, final=final),
        grid=(N_UNITS, per_unit),
        in_specs=in_specs,
        out_specs=out_specs,
        out_shape=out_shape,
        scratch_shapes=[pltpu.VMEM((E * SLOTS, _SC_TOK), BF16)],
        compiler_params=_cparams(("arbitrary", "arbitrary"), 40),
        name="scatter",
    )(*args)


def kernel(x_prompt, x_sample, cache_k, cache_v, c, c_ctx, w_mod, b_mod, norm_mix, norm_ffn, w_in, rpb,
           w_pool, pool_scale, w_out, w_router, w_gate, w_up, w_down, norm_final):
    srcs = [x_prompt.reshape(N_CTX, D), x_sample.reshape(N_LAT, D)]
    cond = jnp.zeros((N_ROW, D), F32).at[0].set(c_ctx).at[1:1 + DEC_BATCH].set(c)
    mods = _adaln(cond, w_mod, b_mod)
    ck = cache_k.reshape(DEC_BATCH, DEPTH, PAST, NA_W)
    cv = cache_v.reshape(DEC_BATCH, DEPTH, PAST, NA_W)

    new_kv = []
    for l in range(DEPTH):
        q, k, v, u, *new_kv = _inproj(srcs, mods, norm_mix, _to_bf16(w_in, l), l, new_kv)
        att = _attention(q, k, v, ck, cv, rpb, l)
        pool = _pool_mixer(u, w_pool, pool_scale, l)

        w_router_pad = jnp.pad(w_router[l], ((0, 0), (0, LANES - E))).astype(BF16)
        y, h, logits = _outproj(srcs, att, pool, mods, norm_ffn, _to_bf16(w_out, l), w_router_pad, l)

        pos, aff = _route(logits)
        xs, gslot = _gather(pos, aff, h)
        ys = _ffn_down(_ffn_up(xs, w_gate, w_up, l), w_down, gslot, l)
        srcs = _scatter(pos, ys, y, mods, l, norm_final if l == DEPTH - 1 else None)
        srcs = list(srcs) if l == DEPTH - 1 else [srcs]

    y_prompt, y_sample = srcs
    new_k, new_v = new_kv
    return (y_prompt.reshape(BATCH, SEQ, D), y_sample.reshape(DEC_BATCH, DEC_SEQ, D),
            new_k.reshape(BATCH, DEPTH, SEQ, H, DH), new_v.reshape(BATCH, DEPTH, SEQ, H, DH))
```

```python
import functools

import jax
import jax.numpy as jnp
from jax import lax
from jax.experimental import pallas as pl
from jax.experimental.pallas import tpu as pltpu

F32 = jnp.float32
BF16 = jnp.bfloat16

D = 2048
BATCH, SEQ = 16, 256
DEPTH = 2
DEC_BATCH, DEC_SEQ = 8, 1024
PAST = 512
GRID_W = 64
GRID_H = DEC_SEQ // GRID_W
H, DH = 16, 64
NA_W = H * DH
POOL_WINDOWS = (2, 4, 8, 16)
PG = 256
POOL_W = 1024
IN_W = 3 * NA_W + POOL_W
NA_ROWS, NA_COLS = 8, 16
E = 16
D_EXP = 1024
N_MOD = 6
RMS_EPS = 1e-6
NEG_INF = -1e30

N_CTX = BATCH * SEQ
N_LAT = DEC_BATCH * DEC_SEQ
N_TOK = N_CTX + N_LAT
UNIT = 1024
N_UNITS = N_TOK // UNIT
CTX_UNITS = N_CTX // UNIT
REQ_PER_CTX_UNIT = UNIT // SEQ
SLOTS = 128
CAP_CTX = 2 * SEQ // E
CAP_LAT = 2 * DEC_SEQ // E
N_SLOT = N_UNITS * SLOTS
N_ROW = 16
LANES = 128
MIB = 1024 * 1024


def _cparams(sem, vmem_mib):
    return pltpu.CompilerParams(dimension_semantics=sem, vmem_limit_bytes=vmem_mib * MIB)


def _resident(block_shape, index_map):
    return pl.BlockSpec(block_shape, index_map, pipeline_mode=pl.Buffered(1))


def _dot(a, b):
    return jnp.dot(a, b, preferred_element_type=F32)


def _dot_nt(a, b):
    return lax.dot_general(a, b, (((1,), (1,)), ((), ())), preferred_element_type=F32)


def _dot_tn(a, b):
    return lax.dot_general(a, b, (((0,), (0,)), ((), ())), preferred_element_type=F32)


def _silu(x):
    return x / (1.0 + jnp.exp(-x))


def _norm_mod(x, gain, shift, scale):
    y = x * lax.rsqrt(jnp.mean(x * x, axis=-1, keepdims=True) + RMS_EPS)
    return (y * gain) * (1.0 + scale) + shift


def _mod_row_of_tile(i, tile):
    per_req = DEC_SEQ // tile
    n_ctx_tiles = N_CTX // tile
    return jnp.where(i < n_ctx_tiles, 0, 1 + (i - n_ctx_tiles) // per_req)


def _mod_spec(layer, which):
    return pl.BlockSpec((None, N_ROW, D), lambda *ids: (layer, 0, which))


def _mod_vec(ref, tile, axis=0):
    return ref[pl.ds(_mod_row_of_tile(pl.program_id(axis), tile), 1), :]


def _gain_spec():
    return pl.BlockSpec((DEPTH, D), lambda *ids: (0, 0))


def _token_specs(srcs, tile, width):
    if len(srcs) == 1:
        return [pl.BlockSpec((tile, width), lambda i: (i, 0))]
    n_ctx_tiles = N_CTX // tile
    return [pl.BlockSpec((tile, width), lambda i: (jnp.minimum(i, n_ctx_tiles - 1), 0)),
            pl.BlockSpec((tile, width), lambda i: (jnp.maximum(i - n_ctx_tiles, 0), 0))]


def _read_tokens(refs, tile, rows=slice(None)):
    if len(refs) == 1:
        return refs[0][rows, :]
    return jnp.where(pl.program_id(0) < N_CTX // tile, refs[0][rows, :], refs[1][rows, :])


def _adaln_kernel(c_ref, w_ref, b_ref, o_ref):
    a = _silu(c_ref[...]).astype(BF16)
    o_ref[0] = _dot(a, w_ref[0].astype(BF16)) + b_ref[pl.ds(pl.program_id(0), 1), :]


def _adaln(cond, w_mod, b_mod):
    tn = 1024
    return pl.pallas_call(
        _adaln_kernel,
        grid=(DEPTH, N_MOD * D // tn),
        in_specs=[pl.BlockSpec((N_ROW, D), lambda l, j: (0, 0)),
                  pl.BlockSpec((1, D, tn), lambda l, j: (l, 0, j)),
                  pl.BlockSpec((DEPTH, tn), lambda l, j: (0, j))],
        out_specs=pl.BlockSpec((1, N_ROW, tn), lambda l, j: (l, 0, j)),
        out_shape=jax.ShapeDtypeStruct((DEPTH, N_ROW, N_MOD * D), F32),
        compiler_params=_cparams(("arbitrary", "arbitrary"), 40),
        name="adaln",
    )(cond, w_mod, b_mod)


def _cast_kernel(w_ref, o_ref):
    o_ref[...] = w_ref[...].astype(BF16)


def _to_bf16(w, layer):
    rows, cols = w.shape[1:]
    tr = 512
    return pl.pallas_call(
        _cast_kernel,
        grid=(rows // tr,),
        in_specs=[pl.BlockSpec((None, tr, cols), lambda i: (layer, i, 0))],
        out_specs=pl.BlockSpec((tr, cols), lambda i: (i, 0)),
        out_shape=jax.ShapeDtypeStruct((rows, cols), BF16),
        compiler_params=_cparams(("arbitrary",), 32),
        name="cast_bf16",
    )(w)


_INPROJ_TILE = 256


def _inproj_kernel(*refs, n_src, n_prev, layer, tm):
    x_refs = refs[:n_src]
    g_ref, sh_ref, sc_ref, w_ref = refs[n_src:n_src + 4]
    q_ref, k_ref, v_ref, u_ref, kf_ref, vf_ref = refs[n_src + 4 + n_prev:]
    for r in range(tm // SEQ):
        rows = slice(r * SEQ, (r + 1) * SEQ)
        h = _norm_mod(_read_tokens(x_refs, tm, rows), g_ref[layer:layer + 1, :], _mod_vec(sh_ref, tm),
                      _mod_vec(sc_ref, tm)).astype(BF16)
        q_ref[rows, :] = (_dot(h, w_ref[:, 0:NA_W]) * (DH ** -0.5)).astype(BF16)
        k = _dot(h, w_ref[:, NA_W:2 * NA_W])
        v = _dot(h, w_ref[:, 2 * NA_W:3 * NA_W])
        k_ref[rows, :] = k.astype(BF16)
        v_ref[rows, :] = v.astype(BF16)
        u_ref[rows, :] = _dot(h, w_ref[:, 3 * NA_W:IN_W])

        @pl.when(pl.program_id(0) < N_CTX // tm)
        def _():
            for slab in range(kf_ref.shape[1]):
                mine = slab == (layer if n_prev == 0 else 0)
                kf_ref[r, slab] = k if mine else jnp.zeros_like(k)
                vf_ref[r, slab] = v if mine else jnp.zeros_like(v)


def _inproj(srcs, mods, norm_gain, w_in_bf, layer, kv_prev):
    tm = _INPROJ_TILE if len(srcs) > 1 else 2 * _INPROJ_TILE
    n_ctx_tiles = N_CTX // tm
    tok = lambda: pl.BlockSpec((tm, NA_W), lambda i: (i, 0))
    if kv_prev:
        ctx_only = lambda: pl.BlockSpec((tm // SEQ, 1, SEQ, NA_W),
                                        lambda i: (jnp.minimum(i, n_ctx_tiles - 1), layer, 0, 0))
    else:
        ctx_only = lambda: pl.BlockSpec((tm // SEQ, DEPTH, SEQ, NA_W),
                                        lambda i: (jnp.minimum(i, n_ctx_tiles - 1), 0, 0, 0))
    n_in = len(srcs) + 4
    return pl.pallas_call(
        functools.partial(_inproj_kernel, n_src=len(srcs), n_prev=len(kv_prev), layer=layer, tm=tm),
        grid=(N_TOK // tm,),
        in_specs=_token_specs(srcs, tm, D) + [
            _gain_spec(), _mod_spec(layer, 0), _mod_spec(layer, 1),
            _resident((D, IN_W), lambda i: (0, 0))] + [pl.BlockSpec(memory_space=pl.ANY)] * len(kv_prev),
        out_specs=[tok(), tok(), tok(), tok(), ctx_only(), ctx_only()],
        out_shape=[jax.ShapeDtypeStruct((N_TOK, NA_W), BF16)] * 3
        + [jax.ShapeDtypeStruct((N_TOK, POOL_W), F32)]
        + [jax.ShapeDtypeStruct((BATCH, DEPTH, SEQ, NA_W), F32)] * 2,
        input_output_aliases={n_in + j: 4 + j for j in range(len(kv_prev))},
        compiler_params=_cparams(("arbitrary",), 56),
        name="inproj",
    )(*srcs, norm_gain, mods, mods, w_in_bf, *kv_prev)


def _na_window_start(qr):
    return min(max(qr - NA_ROWS // 2, 0), GRID_H - NA_ROWS)


def _fill_na_bias(rpb_ref, o_ref):
    assert LANES == 2 * GRID_W
    qc = lax.broadcasted_iota(jnp.int32, (GRID_W, LANES), 0)
    lane = lax.broadcasted_iota(jnp.int32, (GRID_W, LANES), 1)
    kc = jnp.bitwise_and(lane, GRID_W - 1)
    cs = jnp.clip(qc - NA_COLS // 2, 0, GRID_W - NA_COLS)
    in_col = (kc >= cs) & (kc < cs + NA_COLS)
    keep_lanes = {(True, True): in_col, (True, False): in_col & (lane < GRID_W),
                  (False, True): in_col & (lane >= GRID_W)}
    n_dr = 2 * NA_ROWS - 1
    zero_rows = jnp.zeros((GRID_W, LANES), F32)

    def rpb_rows(dr):
        if not 0 <= dr < n_dr:
            return zero_rows
        return jnp.broadcast_to(rpb_ref[dr:dr + 1, :], (GRID_W, LANES))

    pair_cache = {}

    def pair_tile(dr, use_first, use_second):
        if not (use_first or use_second):
            return jnp.full((GRID_W, LANES), NEG_INF, F32)
        key = (dr, use_first, use_second)
        if key not in pair_cache:
            src = rpb_rows(dr) + pltpu.roll(rpb_rows(dr + 1), GRID_W, 1)
            toep = pltpu.roll(src, LANES - (NA_COLS - 1), 1, stride=1, stride_axis=0)
            pair_cache[key] = jnp.where(keep_lanes[(use_first, use_second)], toep, NEG_INF)
        return pair_cache[key]

    for qr in range(GRID_H):
        rs = _na_window_start(qr)
        for kr in range(0, GRID_H, 2):
            use = [rs <= k < rs + NA_ROWS for k in (kr, kr + 1)]
            o_ref[qr * GRID_W:(qr + 1) * GRID_W, kr * GRID_W:(kr + 2) * GRID_W] = pair_tile(
                kr - qr + NA_ROWS - 1, use[0], use[1])


_NA_Q_BLOCK = 4 * GRID_W


def _na_key_range(qb):
    rows_per_block = _NA_Q_BLOCK // GRID_W
    lo = _na_window_start(qb * rows_per_block) * GRID_W
    hi = (_na_window_start((qb + 1) * rows_per_block - 1) + NA_ROWS) * GRID_W
    return (lo // _NA_Q_BLOCK * _NA_Q_BLOCK, -(-hi // _NA_Q_BLOCK) * _NA_Q_BLOCK)


_NA_KEY_RANGES = [_na_key_range(qb) for qb in range(DEC_SEQ // _NA_Q_BLOCK)]


_PAIRS_PER_STEP = 4


def _attn_kernel(q_ref, k_ref, v_ref, ck_ref, cv_ref, rpb_ref, o_ref, kc_ref, vv_ref, tab_ref):
    @pl.when(pl.program_id(1) == CTX_UNITS)
    def _():
        for h in range(2 * _PAIRS_PER_STEP):
            _fill_na_bias(rpb_ref.at[h], tab_ref.at[h])

    lane = lax.broadcasted_iota(jnp.int32, (1, 2 * DH), 1)
    head_lanes = [lane < DH, lane >= DH]

    def probs(scores):
        m = functools.reduce(jnp.maximum, [jnp.max(s, axis=-1, keepdims=True) for s in scores])
        return [jnp.exp(s - m).astype(BF16) for s in scores]

    def merge_heads(acc):
        outs = [a[:, 0:2 * DH] / a[:, 2 * DH:4 * DH] for a in acc]
        return jnp.where(head_lanes[0], outs[0], outs[1]).astype(o_ref.dtype)

    pairs = [(pp, slice(pp * 2 * DH, (pp + 1) * 2 * DH)) for pp in range(_PAIRS_PER_STEP)]
    for pp, pair in pairs:
        vv_ref[pp, 0:UNIT, 0:2 * DH] = v_ref[:, pair]
        vv_ref[pp, :, 2 * DH:4 * DH] = jnp.ones((UNIT + PAST, 2 * DH), BF16)

    def head_queries(rows, pair):
        q = q_ref[rows, pair]
        return [jnp.where(m, q, jnp.zeros_like(q)) for m in head_lanes]

    @pl.when(pl.program_id(1) < CTX_UNITS)
    def _():
        for pp, pair in pairs:
            for r in range(REQ_PER_CTX_UNIT):
                rows = slice(r * SEQ, (r + 1) * SEQ)
                acc = []
                for qh in head_queries(rows, pair):
                    (p,) = probs([_dot_nt(qh, k_ref[rows, pair])])
                    acc.append(_dot(p, vv_ref[pp, rows, :]))
                o_ref[rows, pair] = merge_heads(acc)

    @pl.when(pl.program_id(1) >= CTX_UNITS)
    def _():
        for pp, pair in pairs:
            kc_ref[pp] = ck_ref[0, 0, :, pair].astype(BF16)
            vv_ref[pp, UNIT:UNIT + PAST, 0:2 * DH] = cv_ref[0, 0, :, pair].astype(BF16)
            for qb, (k_lo, k_hi) in enumerate(_NA_KEY_RANGES):
                rows = slice(qb * _NA_Q_BLOCK, (qb + 1) * _NA_Q_BLOCK)
                keys = slice(k_lo, k_hi)
                acc = []
                for hh, qh in enumerate(head_queries(rows, pair)):
                    s_lat = _dot_nt(qh, k_ref[keys, pair]) + tab_ref[2 * pp + hh, rows, keys]
                    s_ctx = _dot_nt(qh, kc_ref[pp])
                    p_lat, p_ctx = probs([s_lat, s_ctx])
                    acc.append(_dot(p_lat, vv_ref[pp, keys, :]) + _dot(p_ctx, vv_ref[pp, UNIT:UNIT + PAST, :]))
                o_ref[rows, pair] = merge_heads(acc)


def _attention(q, k, v, cache_k, cache_v, rpb, layer):
    width = _PAIRS_PER_STEP * 2 * DH
    blk = lambda: pl.BlockSpec((UNIT, width), lambda g, u: (u, g))
    cache = lambda: pl.BlockSpec((1, 1, PAST, width), lambda g, u: (jnp.maximum(u - CTX_UNITS, 0), layer, 0, g))
    rpb_pad = jnp.pad(rpb, ((0, 0), (0, 0), (0, 0), (0, LANES - rpb.shape[-1])))
    return pl.pallas_call(
        _attn_kernel,
        grid=(NA_W // width, N_UNITS),
        in_specs=[blk(), blk(), blk(), cache(), cache(),
                  pl.BlockSpec((None, 2 * _PAIRS_PER_STEP, rpb.shape[2], LANES), lambda g, u: (layer, g, 0, 0))],
        out_specs=blk(),
        out_shape=jax.ShapeDtypeStruct((N_TOK, NA_W), BF16),
        scratch_shapes=[pltpu.VMEM((_PAIRS_PER_STEP, PAST, 2 * DH), BF16),
                        pltpu.VMEM((_PAIRS_PER_STEP, UNIT + PAST, 4 * DH), BF16),
                        pltpu.VMEM((2 * _PAIRS_PER_STEP, DEC_SEQ, DEC_SEQ), F32)],
        compiler_params=_cparams(("arbitrary", "arbitrary"), 56),
        name="attention",
    )(q, k, v, cache_k, cache_v, rpb_pad)


_POOL_EDGE = 8


def _pool_kernel(u_ref, wp_ref, ps_ref, o_ref, *, layer):
    n = jnp.where(pl.program_id(0) < CTX_UNITS, SEQ, DEC_SEQ)
    assert max(POOL_WINDOWS) // 2 <= _POOL_EDGE
    blocks = range(UNIT // SEQ)

    def edge_pos(first_row):
        t = lax.broadcasted_iota(jnp.int32, (_POOL_EDGE, PG), 0) + first_row
        return jnp.bitwise_and(t, n - 1)

    head_pos = [edge_pos(k * SEQ) for k in blocks]
    tail_pos = [edge_pos((k + 1) * SEQ - _POOL_EDGE) for k in blocks]

    def patch(x, head_fn=None, tail_fn=None, body_fn=None):
        pieces = []
        for k in blocks:
            lo, hi = k * SEQ, (k + 1) * SEQ
            head, body, tail = x[lo:lo + _POOL_EDGE], x[lo + _POOL_EDGE:hi - _POOL_EDGE], x[hi - _POOL_EDGE:hi]
            pieces += [head_fn(head, k) if head_fn else head, body_fn(body) if body_fn else body,
                       tail_fn(tail, k) if tail_fn else tail]
        return jnp.concatenate(pieces, axis=0)

    def fwd(a, s):
        return patch(pltpu.roll(a, UNIT - s, 0), tail_fn=lambda x, k: jnp.where(tail_pos[k] < n - s, x, 0.0))

    def bwd(a, s):
        return patch(pltpu.roll(a, s, 0), head_fn=lambda x, k: jnp.where(head_pos[k] >= s, x, 0.0))

    for gi, w in enumerate(POOL_WINDOWS):
        half = w // 2
        cols = slice(gi * PG, (gi + 1) * PG)
        g = u_ref[:, cols]
        f, b, s = g, g, 1
        while s < half:
            f = f + fwd(f, s)
            b = b + bwd(b, s)
            s *= 2
        tot = f + bwd(b, 1)

        def edge_mean(x, pos):
            cnt = jnp.minimum(pos + half, n) - jnp.maximum(pos - half, 0)
            return x / cnt.astype(F32)

        mean = patch(tot, head_fn=lambda x, k: edge_mean(x, head_pos[k]),
                     tail_fn=lambda x, k: edge_mean(x, tail_pos[k]), body_fn=lambda x: x * (1.0 / w))
        dlt = mean - g
        out = _dot(dlt.astype(BF16), wp_ref[gi].astype(BF16)) * ps_ref[layer:layer + 1, cols]
        o_ref[:, cols] = out.astype(o_ref.dtype)


def _pool_mixer(u, w_pool, pool_scale, layer):
    return pl.pallas_call(
        functools.partial(_pool_kernel, layer=layer),
        grid=(N_UNITS,),
        in_specs=[pl.BlockSpec((UNIT, POOL_W), lambda i: (i, 0)),
                  pl.BlockSpec((None, len(POOL_WINDOWS), PG, PG), lambda i: (layer, 0, 0, 0)),
                  pl.BlockSpec((DEPTH, POOL_W), lambda i: (0, 0))],
        out_specs=pl.BlockSpec((UNIT, POOL_W), lambda i: (i, 0)),
        out_shape=jax.ShapeDtypeStruct((N_TOK, POOL_W), BF16),
        compiler_params=_cparams(("arbitrary",), 40),
        name="pool",
    )(u, w_pool, pool_scale)


_OUTPROJ_TILE = 512
_OUTPROJ_ROWS = 256


def _outproj_kernel(*refs, n_src, layer):
    x_refs = refs[:n_src]
    att_ref, pool_ref, w_ref, gate_ref, g2_ref, sh_ref, sc_ref, wr_ref, y_ref, h_ref, lg_ref = refs[n_src:]
    tm = _OUTPROJ_TILE
    for r in range(tm // _OUTPROJ_ROWS):
        rows = slice(r * _OUTPROJ_ROWS, (r + 1) * _OUTPROJ_ROWS)
        mix = _dot(att_ref[rows, :], w_ref[0:NA_W, :]) + _dot(pool_ref[rows, :], w_ref[NA_W:NA_W + POOL_W, :])
        y = _read_tokens(x_refs, tm, rows) + _mod_vec(gate_ref, tm) * mix
        y_ref[rows, :] = y
        h = _norm_mod(y, g2_ref[layer:layer + 1, :], _mod_vec(sh_ref, tm), _mod_vec(sc_ref, tm)).astype(BF16)
        h_ref[rows, :] = h
        lg_ref[rows, :] = _dot(h, wr_ref[...])


def _outproj(srcs, att, pool, mods, norm_gain, w_out_bf, w_router_pad, layer):
    tm = _OUTPROJ_TILE
    row = lambda width: pl.BlockSpec((tm, width), lambda i: (i, 0))
    return pl.pallas_call(
        functools.partial(_outproj_kernel, n_src=len(srcs), layer=layer),
        grid=(N_TOK // tm,),
        in_specs=_token_specs(srcs, tm, D) + [
            row(NA_W), row(POOL_W),
            _resident((D, D), lambda i: (0, 0)),
            _mod_spec(layer, 2), _gain_spec(), _mod_spec(layer, 3), _mod_spec(layer, 4),
            _resident((D, LANES), lambda i: (0, 0))],
        out_specs=[row(D), row(D), row(LANES)],
        out_shape=[jax.ShapeDtypeStruct((N_TOK, D), F32),
                   jax.ShapeDtypeStruct((N_TOK, D), BF16),
                   jax.ShapeDtypeStruct((N_TOK, LANES), F32)],
        compiler_params=_cparams(("arbitrary",), 52),
        name="outproj",
    )(*srcs, att, pool, w_out_bf, mods, norm_gain, mods, mods, w_router_pad)


_SEARCH_BITS = 3


def _choose_slots(aff, n_req, n_tok, cap):
    rows = jnp.concatenate([aff[:, r * n_tok:(r + 1) * n_tok] for r in range(n_req)], axis=0)
    n_rows = n_req * E
    thr = jnp.zeros((n_rows, 1), jnp.int32)
    hi = 31
    while hi > 0:
        lo = max(hi - _SEARCH_BITS, 0)
        n_cand = (1 << (hi - lo)) - 1
        hits = jnp.concatenate([(rows >= lax.bitcast_convert_type(thr | (j << lo), F32)).astype(F32)
                                for j in range(1, n_cand + 1)], axis=0)
        reached = (jnp.sum(hits, axis=-1, keepdims=True) >= cap).astype(jnp.int32)
        group = functools.reduce(jnp.add, [reached[j * n_rows:(j + 1) * n_rows] for j in range(n_cand)])
        thr = thr | lax.shift_left(group, jnp.int32(lo))
        hi = lo
    thr_val = lax.bitcast_convert_type(thr, F32)
    gt = rows > thr_val
    eq = rows == thr_val
    n_gt = jnp.sum(gt.astype(F32), axis=-1, keepdims=True)
    before = (lax.broadcasted_iota(jnp.int32, (n_tok, n_tok), 0)
              < lax.broadcasted_iota(jnp.int32, (n_tok, n_tok), 1)).astype(BF16)
    eq_rank = _dot(eq.astype(BF16), before)
    sel = gt | (eq & (eq_rank < cap - n_gt))
    slot = _dot(sel.astype(BF16), before)
    pieces = [jnp.where(sel[r * E:(r + 1) * E], slot[r * E:(r + 1) * E] + r * cap, -1.0) for r in range(n_req)]
    return jnp.concatenate(pieces, axis=1) if n_req > 1 else pieces[0]


def _route_kernel(lg_ref, pos_ref, aff_ref):
    lg = lg_ref[...].T[0:E, :]
    ex = jnp.exp(lg - jnp.max(lg, axis=0, keepdims=True))
    aff = ex / jnp.sum(ex, axis=0, keepdims=True)
    aff_ref[0] = aff

    @pl.when(pl.program_id(0) < CTX_UNITS)
    def _():
        pos_ref[0] = _choose_slots(aff, REQ_PER_CTX_UNIT, SEQ, CAP_CTX)

    @pl.when(pl.program_id(0) >= CTX_UNITS)
    def _():
        pos_ref[0] = _choose_slots(aff, 1, DEC_SEQ, CAP_LAT)


def _route(logits):
    out_blk = lambda: pl.BlockSpec((1, E, UNIT), lambda u: (u, 0, 0))
    return pl.pallas_call(
        _route_kernel,
        grid=(N_UNITS,),
        in_specs=[pl.BlockSpec((UNIT, LANES), lambda u: (u, 0))],
        out_specs=[out_blk(), out_blk()],
        out_shape=[jax.ShapeDtypeStruct((N_UNITS, E, UNIT), F32)] * 2,
        compiler_params=_cparams(("arbitrary",), 32),
        name="route",
    )(logits)


_COL_CHUNK = 512


def _onehot_rows(pos, p_ref, n_slots, first_slot):
    width = pos.shape[1]
    slot_id = (lax.broadcasted_iota(jnp.int32, (n_slots, width), 0) + first_slot).astype(F32)
    matches = []
    for e in range(E):
        match = pos[e:e + 1, :] == slot_id
        p_ref[e * n_slots:(e + 1) * n_slots, 0:width] = jnp.where(match, 1.0, 0.0).astype(BF16)
        matches.append(match)
    return matches


def _gather_kernel(pos_ref, aff_ref, h_ref, xs_ref, gs_ref, p_ref):
    def gather(tokens, slots, first_slot):
        n = slots.stop - slots.start
        aff = aff_ref[0, :, tokens]
        matches = _onehot_rows(pos_ref[0, :, tokens], p_ref, n, first_slot)
        for e in range(E):
            gate = jnp.sum(jnp.where(matches[e], aff[e:e + 1, :], 0.0), axis=-1, keepdims=True)
            gs_ref[e, slots, :] = jnp.broadcast_to(gate, (n, LANES))
        width = tokens.stop - tokens.start
        for c in range(D // _COL_CHUNK):
            cols = slice(c * _COL_CHUNK, (c + 1) * _COL_CHUNK)
            r = _dot(p_ref[0:E * n, 0:width], h_ref[tokens, cols])
            xs_ref[:, slots, cols] = r.reshape(E, n, _COL_CHUNK).astype(BF16)

    @pl.when(pl.program_id(0) < CTX_UNITS)
    def _():
        for r in range(REQ_PER_CTX_UNIT):
            gather(slice(r * SEQ, (r + 1) * SEQ), slice(r * CAP_CTX, (r + 1) * CAP_CTX), r * CAP_CTX)

    @pl.when(pl.program_id(0) >= CTX_UNITS)
    def _():
        gather(slice(0, UNIT), slice(0, SLOTS), 0)


def _gather(pos, aff, h):
    unit3 = lambda: pl.BlockSpec((1, E, UNIT), lambda u: (u, 0, 0))
    return pl.pallas_call(
        _gather_kernel,
        grid=(N_UNITS,),
        in_specs=[unit3(), unit3(), pl.BlockSpec((UNIT, D), lambda u: (u, 0))],
        out_specs=[pl.BlockSpec((E, SLOTS, D), lambda u: (0, u, 0)),
                   pl.BlockSpec((E, SLOTS, LANES), lambda u: (0, u, 0))],
        out_shape=[jax.ShapeDtypeStruct((E, N_SLOT, D), BF16),
                   jax.ShapeDtypeStruct((E, N_SLOT, LANES), F32)],
        scratch_shapes=[pltpu.VMEM((E * SLOTS, UNIT), BF16)],
        compiler_params=_cparams(("arbitrary",), 48),
        name="gather",
    )(pos, aff, h)


_FFN_ROWS = 256
_FFN_COLS = 256


def _ffn_up_kernel(xs_ref, wg_ref, wu_ref, o_ref):
    for c in range(wg_ref.shape[1] // _FFN_COLS):
        cols = slice(c * _FFN_COLS, (c + 1) * _FFN_COLS)
        wg = wg_ref[:, cols].astype(BF16)
        wu = wu_ref[:, cols].astype(BF16)
        for m in range(N_SLOT // _FFN_ROWS):
            rows = slice(m * _FFN_ROWS, (m + 1) * _FFN_ROWS)
            x = xs_ref[rows, :]
            o_ref[rows, cols] = (_silu(_dot(x, wg)) * _dot(x, wu)).astype(o_ref.dtype)


def _ffn_up(xs, w_gate, w_up, layer):
    tf = 512
    wspec = lambda: pl.BlockSpec((None, None, D, tf), lambda e, j: (layer, e, 0, j))
    return pl.pallas_call(
        _ffn_up_kernel,
        grid=(E, D_EXP // tf),
        in_specs=[pl.BlockSpec((None, N_SLOT, D), lambda e, j: (e, 0, 0)), wspec(), wspec()],
        out_specs=pl.BlockSpec((None, N_SLOT, tf), lambda e, j: (e, 0, j)),
        out_shape=jax.ShapeDtypeStruct((E, N_SLOT, D_EXP), BF16),
        compiler_params=_cparams(("arbitrary", "arbitrary"), 48),
        name="ffn_up",
    )(xs, w_gate, w_up)


def _ffn_down_kernel(h_ref, wd_ref, gs_ref, o_ref):
    for c in range(wd_ref.shape[1] // _FFN_COLS):
        cols = slice(c * _FFN_COLS, (c + 1) * _FFN_COLS)
        wd = wd_ref[:, cols].astype(BF16)
        for m in range(N_SLOT // _FFN_ROWS):
            rows = slice(m * _FFN_ROWS, (m + 1) * _FFN_ROWS)
            gate = jnp.concatenate([gs_ref[rows, :]] * (_FFN_COLS // LANES), axis=1)
            o_ref[rows, cols] = (_dot(h_ref[rows, :], wd) * gate).astype(o_ref.dtype)


def _ffn_down(hcur, w_down, gslot, layer):
    return pl.pallas_call(
        _ffn_down_kernel,
        grid=(E,),
        in_specs=[pl.BlockSpec((None, N_SLOT, D_EXP), lambda e: (e, 0, 0)),
                  pl.BlockSpec((None, None, D_EXP, D), lambda e: (layer, e, 0, 0)),
                  pl.BlockSpec((None, N_SLOT, LANES), lambda e: (e, 0, 0))],
        out_specs=pl.BlockSpec((None, N_SLOT, D), lambda e: (e, 0, 0)),
        out_shape=jax.ShapeDtypeStruct((E, N_SLOT, D), BF16),
        compiler_params=_cparams(("arbitrary",), 48),
        name="ffn_down",
    )(hcur, w_down, gslot)


_SC_TOK = 512


def _scatter_kernel(pos_ref, ys_ref, y_ref, gate_ref, *rest, final):
    if final:
        gain_ref, oc_ref, ol_ref, p_ref = rest
    else:
        o_ref, p_ref = rest

    def emit(rows, updates, is_ctx):
        y = y_ref[rows, :] + _mod_vec(gate_ref, UNIT) * jnp.concatenate(updates, axis=1)
        if not final:
            o_ref[rows, :] = y
        else:
            r = y * lax.rsqrt(jnp.mean(y * y, axis=-1, keepdims=True) + RMS_EPS) * gain_ref[...]
            (oc_ref if is_ctx else ol_ref)[rows, :] = r

    col_chunks = [slice(c * _COL_CHUNK, (c + 1) * _COL_CHUNK) for c in range(D // _COL_CHUNK)]

    @pl.when(pl.program_id(0) < CTX_UNITS)
    def _():
        for r in range(_SC_TOK // SEQ):
            rows = slice(r * SEQ, (r + 1) * SEQ)
            first = pl.multiple_of((pl.program_id(1) * (_SC_TOK // SEQ) + r) * CAP_CTX, CAP_CTX)
            p_rows = p_ref.at[r * E * CAP_CTX:(r + 1) * E * CAP_CTX]
            _onehot_rows(pos_ref[0, :, rows], p_rows, CAP_CTX, first)
            emit(rows, [_dot_tn(p_rows[:, 0:SEQ],
                                ys_ref[:, pl.ds(first, CAP_CTX), cols].reshape(E * CAP_CTX, _COL_CHUNK))
                        for cols in col_chunks], True)

    @pl.when(pl.program_id(0) >= CTX_UNITS)
    def _():
        _onehot_rows(pos_ref[0], p_ref, SLOTS, 0)
        emit(slice(0, _SC_TOK), [_dot_tn(p_ref[...], ys_ref[:, :, cols].reshape(E * SLOTS, _COL_CHUNK))
                                 for cols in col_chunks], False)


def _scatter(pos, ys, y, mods, layer, final_gain=None):
    assert _SC_TOK % SEQ == 0 and UNIT % _SC_TOK == 0
    per_unit = UNIT // _SC_TOK
    n_ctx_tiles = N_CTX // _SC_TOK
    final = final_gain is not None
    tok = lambda: pl.BlockSpec((_SC_TOK, D), lambda u, s: (u * per_unit + s, 0))
    in_specs = [pl.BlockSpec((1, E, _SC_TOK), lambda u, s: (u, 0, s)),
                pl.BlockSpec((E, SLOTS, D), lambda u, s: (0, u, 0)),
                tok(),
                _mod_spec(layer, 5)]
    args = [pos, ys, y, mods]
    if final:
        in_specs.append(pl.BlockSpec((1, D), lambda u, s: (0, 0)))
        args.append(final_gain.reshape(1, D))
        out_specs = [pl.BlockSpec((_SC_TOK, D), lambda u, s: (jnp.minimum(u * per_unit + s, n_ctx_tiles - 1), 0)),
                     pl.BlockSpec((_SC_TOK, D), lambda u, s: (jnp.maximum(u * per_unit + s - n_ctx_tiles, 0), 0))]
        out_shape = [jax.ShapeDtypeStruct((N_CTX, D), F32), jax.ShapeDtypeStruct((N_LAT, D), F32)]
    else:
        out_specs = tok()
        out_shape = jax.ShapeDtypeStruct((N_TOK, D), F32)
    return pl.pallas_call(
        functools.partial(_scatter_kernel, final=final),
        grid=(N_UNITS, per_unit),
        in_specs=in_specs,
        out_specs=out_specs,
        out_shape=out_shape,
        scratch_shapes=[pltpu.VMEM((E * SLOTS, _SC_TOK), BF16)],
        compiler_params=_cparams(("arbitrary", "arbitrary"), 56),
        name="scatter",
    )(*args)


def kernel(x_prompt, x_sample, cache_k, cache_v, c, c_ctx, w_mod, b_mod, norm_mix, norm_ffn, w_in, rpb,
           w_pool, pool_scale, w_out, w_router, w_gate, w_up, w_down, norm_final):
    srcs = [x_prompt.reshape(N_CTX, D), x_sample.reshape(N_LAT, D)]
    cond = jnp.zeros((N_ROW, D), F32).at[0].set(c_ctx).at[1:1 + DEC_BATCH].set(c)
    mods = _adaln(cond, w_mod, b_mod)
    ck = cache_k.reshape(DEC_BATCH, DEPTH, PAST, NA_W)
    cv = cache_v.reshape(DEC_BATCH, DEPTH, PAST, NA_W)

    new_kv = []
    for l in range(DEPTH):
        q, k, v, u, *new_kv = _inproj(srcs, mods, norm_mix, _to_bf16(w_in, l), l, new_kv)
        att = _attention(q, k, v, ck, cv, rpb, l)
        pool = _pool_mixer(u, w_pool, pool_scale, l)

        w_router_pad = jnp.pad(w_router[l], ((0, 0), (0, LANES - E))).astype(BF16)
        y, h, logits = _outproj(srcs, att, pool, mods, norm_ffn, _to_bf16(w_out, l), w_router_pad, l)

        pos, aff = _route(logits)
        xs, gslot = _gather(pos, aff, h)
        ys = _ffn_down(_ffn_up(xs, w_gate, w_up, l), w_down, gslot, l)
        srcs = _scatter(pos, ys, y, mods, l, norm_final if l == DEPTH - 1 else None)
        srcs = list(srcs) if l == DEPTH - 1 else [srcs]

    y_prompt, y_sample = srcs
    new_k, new_v = new_kv
    return (y_prompt.reshape(BATCH, SEQ, D), y_sample.reshape(DEC_BATCH, DEC_SEQ, D),
            new_k.reshape(BATCH, DEPTH, SEQ, H, DH), new_v.reshape(BATCH, DEPTH, SEQ, H, DH))
```

```python
import functools

import jax
import jax.numpy as jnp
from jax import lax
from jax.experimental import pallas as pl
from jax.experimental.pallas import tpu as pltpu

F32 = jnp.float32
BF16 = jnp.bfloat16

D = 2048
BATCH, SEQ = 16, 256
DEPTH = 2
DEC_BATCH, DEC_SEQ = 8, 1024
PAST = 512
GRID_W = 64
GRID_H = DEC_SEQ // GRID_W
H, DH = 16, 64
NA_W = H * DH
POOL_WINDOWS = (2, 4, 8, 16)
PG = 256
POOL_W = 1024
IN_W = 3 * NA_W + POOL_W
NA_ROWS, NA_COLS = 8, 16
E = 16
D_EXP = 1024
N_MOD = 6
RMS_EPS = 1e-6
NEG_INF = -1e30

N_CTX = BATCH * SEQ
N_LAT = DEC_BATCH * DEC_SEQ
N_TOK = N_CTX + N_LAT
UNIT = 1024
N_UNITS = N_TOK // UNIT
CTX_UNITS = N_CTX // UNIT
REQ_PER_CTX_UNIT = UNIT // SEQ
SLOTS = 128
CAP_CTX = 2 * SEQ // E
CAP_LAT = 2 * DEC_SEQ // E
N_SLOT = N_UNITS * SLOTS
N_ROW = 16
LANES = 128
MIB = 1024 * 1024


def _cparams(sem, vmem_mib):
    return pltpu.CompilerParams(dimension_semantics=sem, vmem_limit_bytes=vmem_mib * MIB)


def _resident(block_shape, index_map):
    return pl.BlockSpec(block_shape, index_map, pipeline_mode=pl.Buffered(1))


def _dot(a, b):
    return jnp.dot(a, b, preferred_element_type=F32)


def _dot_nt(a, b):
    return lax.dot_general(a, b, (((1,), (1,)), ((), ())), preferred_element_type=F32)


def _dot_tn(a, b):
    return lax.dot_general(a, b, (((0,), (0,)), ((), ())), preferred_element_type=F32)


def _silu(x):
    return x / (1.0 + jnp.exp(-x))


def _norm_mod(x, gain, shift, scale):
    y = x * lax.rsqrt(jnp.mean(x * x, axis=-1, keepdims=True) + RMS_EPS)
    return (y * gain) * (1.0 + scale) + shift


def _mod_row_of_tile(i, tile):
    per_req = DEC_SEQ // tile
    n_ctx_tiles = N_CTX // tile
    return jnp.where(i < n_ctx_tiles, 0, 1 + (i - n_ctx_tiles) // per_req)


def _mod_spec(layer, which):
    return pl.BlockSpec((None, N_ROW, D), lambda *ids: (layer, 0, which))


def _mod_vec(ref, tile, axis=0):
    return ref[pl.ds(_mod_row_of_tile(pl.program_id(axis), tile), 1), :]


def _gain_spec():
    return pl.BlockSpec((DEPTH, D), lambda *ids: (0, 0))


def _token_specs(srcs, tile, width):
    if len(srcs) == 1:
        return [pl.BlockSpec((tile, width), lambda i: (i, 0))]
    n_ctx_tiles = N_CTX // tile
    return [pl.BlockSpec((tile, width), lambda i: (jnp.minimum(i, n_ctx_tiles - 1), 0)),
            pl.BlockSpec((tile, width), lambda i: (jnp.maximum(i - n_ctx_tiles, 0), 0))]


def _read_tokens(refs, tile, rows=slice(None)):
    if len(refs) == 1:
        return refs[0][rows, :]
    return jnp.where(pl.program_id(0) < N_CTX // tile, refs[0][rows, :], refs[1][rows, :])


def _adaln_kernel(c_ref, w_ref, b_ref, o_ref):
    a = _silu(c_ref[...]).astype(BF16)
    o_ref[0] = _dot(a, w_ref[0].astype(BF16)) + b_ref[pl.ds(pl.program_id(0), 1), :]


def _adaln(cond, w_mod, b_mod):
    tn = 1024
    return pl.pallas_call(
        _adaln_kernel,
        grid=(DEPTH, N_MOD * D // tn),
        in_specs=[pl.BlockSpec((N_ROW, D), lambda l, j: (0, 0)),
                  pl.BlockSpec((1, D, tn), lambda l, j: (l, 0, j)),
                  pl.BlockSpec((DEPTH, tn), lambda l, j: (0, j))],
        out_specs=pl.BlockSpec((1, N_ROW, tn), lambda l, j: (l, 0, j)),
        out_shape=jax.ShapeDtypeStruct((DEPTH, N_ROW, N_MOD * D), F32),
        compiler_params=_cparams(("arbitrary", "arbitrary"), 40),
        name="adaln",
    )(cond, w_mod, b_mod)


def _cast_kernel(w_ref, o_ref):
    o_ref[...] = w_ref[...].astype(BF16)


def _to_bf16(w, layer):
    rows, cols = w.shape[1:]
    tr = 512
    return pl.pallas_call(
        _cast_kernel,
        grid=(rows // tr,),
        in_specs=[pl.BlockSpec((None, tr, cols), lambda i: (layer, i, 0))],
        out_specs=pl.BlockSpec((tr, cols), lambda i: (i, 0)),
        out_shape=jax.ShapeDtypeStruct((rows, cols), BF16),
        compiler_params=_cparams(("arbitrary",), 32),
        name="cast_bf16",
    )(w)


_INPROJ_TILE = 256


def _inproj_kernel(*refs, n_src, n_prev, layer, tm):
    x_refs = refs[:n_src]
    g_ref, sh_ref, sc_ref, w_ref = refs[n_src:n_src + 4]
    q_ref, k_ref, v_ref, u_ref, kf_ref, vf_ref = refs[n_src + 4 + n_prev:]
    for r in range(tm // SEQ):
        rows = slice(r * SEQ, (r + 1) * SEQ)
        h = _norm_mod(_read_tokens(x_refs, tm, rows), g_ref[layer:layer + 1, :], _mod_vec(sh_ref, tm),
                      _mod_vec(sc_ref, tm)).astype(BF16)
        q_ref[rows, :] = (_dot(h, w_ref[:, 0:NA_W]) * (DH ** -0.5)).astype(BF16)
        k = _dot(h, w_ref[:, NA_W:2 * NA_W])
        v = _dot(h, w_ref[:, 2 * NA_W:3 * NA_W])
        k_ref[rows, :] = k.astype(BF16)
        v_ref[rows, :] = v.astype(BF16)
        u_ref[rows, :] = _dot(h, w_ref[:, 3 * NA_W:IN_W])

        @pl.when(pl.program_id(0) < N_CTX // tm)
        def _():
            for slab in range(kf_ref.shape[1]):
                mine = slab == (layer if n_prev == 0 else 0)
                kf_ref[r, slab] = k if mine else jnp.zeros_like(k)
                vf_ref[r, slab] = v if mine else jnp.zeros_like(v)


def _inproj(srcs, mods, norm_gain, w_in_bf, layer, kv_prev):
    tm = _INPROJ_TILE if len(srcs) > 1 else 2 * _INPROJ_TILE
    n_ctx_tiles = N_CTX // tm
    tok = lambda: pl.BlockSpec((tm, NA_W), lambda i: (i, 0))
    if kv_prev:
        ctx_only = lambda: pl.BlockSpec((tm // SEQ, 1, SEQ, NA_W),
                                        lambda i: (jnp.minimum(i, n_ctx_tiles - 1), layer, 0, 0))
    else:
        ctx_only = lambda: pl.BlockSpec((tm // SEQ, DEPTH, SEQ, NA_W),
                                        lambda i: (jnp.minimum(i, n_ctx_tiles - 1), 0, 0, 0))
    n_in = len(srcs) + 4
    return pl.pallas_call(
        functools.partial(_inproj_kernel, n_src=len(srcs), n_prev=len(kv_prev), layer=layer, tm=tm),
        grid=(N_TOK // tm,),
        in_specs=_token_specs(srcs, tm, D) + [
            _gain_spec(), _mod_spec(layer, 0), _mod_spec(layer, 1),
            _resident((D, IN_W), lambda i: (0, 0))] + [pl.BlockSpec(memory_space=pl.ANY)] * len(kv_prev),
        out_specs=[tok(), tok(), tok(), tok(), ctx_only(), ctx_only()],
        out_shape=[jax.ShapeDtypeStruct((N_TOK, NA_W), BF16)] * 3
        + [jax.ShapeDtypeStruct((N_TOK, POOL_W), F32)]
        + [jax.ShapeDtypeStruct((BATCH, DEPTH, SEQ, NA_W), F32)] * 2,
        input_output_aliases={n_in + j: 4 + j for j in range(len(kv_prev))},
        compiler_params=_cparams(("arbitrary",), 56),
        name="inproj",
    )(*srcs, norm_gain, mods, mods, w_in_bf, *kv_prev)


def _na_window_start(qr):
    return min(max(qr - NA_ROWS // 2, 0), GRID_H - NA_ROWS)


def _fill_na_bias(rpb_ref, o_ref):
    assert LANES == 2 * GRID_W
    qc = lax.broadcasted_iota(jnp.int32, (GRID_W, LANES), 0)
    lane = lax.broadcasted_iota(jnp.int32, (GRID_W, LANES), 1)
    kc = jnp.bitwise_and(lane, GRID_W - 1)
    cs = jnp.clip(qc - NA_COLS // 2, 0, GRID_W - NA_COLS)
    in_col = (kc >= cs) & (kc < cs + NA_COLS)
    keep_lanes = {(True, True): in_col, (True, False): in_col & (lane < GRID_W),
                  (False, True): in_col & (lane >= GRID_W)}
    n_dr = 2 * NA_ROWS - 1
    zero_rows = jnp.zeros((GRID_W, LANES), F32)

    def rpb_rows(dr):
        if not 0 <= dr < n_dr:
            return zero_rows
        return jnp.broadcast_to(rpb_ref[dr:dr + 1, :], (GRID_W, LANES))

    pair_cache = {}

    def pair_tile(dr, use_first, use_second):
        if not (use_first or use_second):
            return jnp.full((GRID_W, LANES), NEG_INF, F32)
        key = (dr, use_first, use_second)
        if key not in pair_cache:
            src = rpb_rows(dr) + pltpu.roll(rpb_rows(dr + 1), GRID_W, 1)
            toep = pltpu.roll(src, LANES - (NA_COLS - 1), 1, stride=1, stride_axis=0)
            pair_cache[key] = jnp.where(keep_lanes[(use_first, use_second)], toep, NEG_INF)
        return pair_cache[key]

    for qr in range(GRID_H):
        rs = _na_window_start(qr)
        for kr in range(0, GRID_H, 2):
            use = [rs <= k < rs + NA_ROWS for k in (kr, kr + 1)]
            o_ref[qr * GRID_W:(qr + 1) * GRID_W, kr * GRID_W:(kr + 2) * GRID_W] = pair_tile(
                kr - qr + NA_ROWS - 1, use[0], use[1])


_NA_Q_BLOCK = 4 * GRID_W


def _na_key_range(qb):
    rows_per_block = _NA_Q_BLOCK // GRID_W
    lo = _na_window_start(qb * rows_per_block) * GRID_W
    hi = (_na_window_start((qb + 1) * rows_per_block - 1) + NA_ROWS) * GRID_W
    return (lo // _NA_Q_BLOCK * _NA_Q_BLOCK, -(-hi // _NA_Q_BLOCK) * _NA_Q_BLOCK)


_NA_KEY_RANGES = [_na_key_range(qb) for qb in range(DEC_SEQ // _NA_Q_BLOCK)]


_PAIRS_PER_STEP = 4


def _attn_kernel(q_ref, k_ref, v_ref, ck_ref, cv_ref, rpb_ref, o_ref, vv_ref, tab_ref):
    @pl.when(pl.program_id(1) == CTX_UNITS)
    def _():
        for h in range(2 * _PAIRS_PER_STEP):
            _fill_na_bias(rpb_ref.at[h], tab_ref.at[h])

    lane = lax.broadcasted_iota(jnp.int32, (1, 2 * DH), 1)
    head_lanes = [lane < DH, lane >= DH]

    def probs(scores):
        m = functools.reduce(jnp.maximum, [jnp.max(s, axis=-1, keepdims=True) for s in scores])
        return [jnp.exp(s - m).astype(BF16) for s in scores]

    def merge_heads(acc):
        outs = [a[:, 0:2 * DH] / a[:, 2 * DH:4 * DH] for a in acc]
        return jnp.where(head_lanes[0], outs[0], outs[1]).astype(o_ref.dtype)

    pairs = [(pp, slice(pp * 2 * DH, (pp + 1) * 2 * DH)) for pp in range(_PAIRS_PER_STEP)]
    for pp, pair in pairs:
        vv_ref[pp, 0:UNIT, 0:2 * DH] = v_ref[:, pair]
        vv_ref[pp, :, 2 * DH:4 * DH] = jnp.ones((UNIT + PAST, 2 * DH), BF16)

    def head_queries(rows, pair):
        q = q_ref[rows, pair]
        return [jnp.where(m, q, jnp.zeros_like(q)) for m in head_lanes]

    @pl.when(pl.program_id(1) < CTX_UNITS)
    def _():
        for pp, pair in pairs:
            for r in range(REQ_PER_CTX_UNIT):
                rows = slice(r * SEQ, (r + 1) * SEQ)
                acc = []
                for qh in head_queries(rows, pair):
                    (p,) = probs([_dot_nt(qh, k_ref[rows, pair])])
                    acc.append(_dot(p, vv_ref[pp, rows, :]))
                o_ref[rows, pair] = merge_heads(acc)

    @pl.when(pl.program_id(1) >= CTX_UNITS)
    def _():
        for pp, pair in pairs:
            vv_ref[pp, UNIT:UNIT + PAST, 0:2 * DH] = cv_ref[0, 0, :, pair]
            for qb, (k_lo, k_hi) in enumerate(_NA_KEY_RANGES):
                rows = slice(qb * _NA_Q_BLOCK, (qb + 1) * _NA_Q_BLOCK)
                keys = slice(k_lo, k_hi)
                acc = []
                for hh, qh in enumerate(head_queries(rows, pair)):
                    s_lat = _dot_nt(qh, k_ref[keys, pair]) + tab_ref[2 * pp + hh, rows, keys]
                    s_ctx = _dot_nt(qh, ck_ref[0, 0, :, pair])
                    p_lat, p_ctx = probs([s_lat, s_ctx])
                    acc.append(_dot(p_lat, vv_ref[pp, keys, :]) + _dot(p_ctx, vv_ref[pp, UNIT:UNIT + PAST, :]))
                o_ref[rows, pair] = merge_heads(acc)


def _attention(q, k, v, cache_k, cache_v, rpb, layer):
    width = _PAIRS_PER_STEP * 2 * DH
    blk = lambda: pl.BlockSpec((UNIT, width), lambda g, u: (u, g))
    cache = lambda: pl.BlockSpec((1, 1, PAST, width), lambda g, u: (jnp.maximum(u - CTX_UNITS, 0), layer, 0, g))
    rpb_pad = jnp.pad(rpb, ((0, 0), (0, 0), (0, 0), (0, LANES - rpb.shape[-1])))
    return pl.pallas_call(
        _attn_kernel,
        grid=(NA_W // width, N_UNITS),
        in_specs=[blk(), blk(), blk(), cache(), cache(),
                  pl.BlockSpec((None, 2 * _PAIRS_PER_STEP, rpb.shape[2], LANES), lambda g, u: (layer, g, 0, 0))],
        out_specs=blk(),
        out_shape=jax.ShapeDtypeStruct((N_TOK, NA_W), BF16),
        scratch_shapes=[pltpu.VMEM((_PAIRS_PER_STEP, UNIT + PAST, 4 * DH), BF16),
                        pltpu.VMEM((2 * _PAIRS_PER_STEP, DEC_SEQ, DEC_SEQ), F32)],
        compiler_params=_cparams(("arbitrary", "arbitrary"), 56),
        name="attention",
    )(q, k, v, cache_k, cache_v, rpb_pad)


_POOL_EDGE = 8


def _pool_kernel(u_ref, wp_ref, ps_ref, o_ref, *, layer):
    n = jnp.where(pl.program_id(0) < CTX_UNITS, SEQ, DEC_SEQ)
    assert max(POOL_WINDOWS) // 2 <= _POOL_EDGE
    blocks = range(UNIT // SEQ)

    def edge_pos(first_row):
        t = lax.broadcasted_iota(jnp.int32, (_POOL_EDGE, PG), 0) + first_row
        return jnp.bitwise_and(t, n - 1)

    head_pos = [edge_pos(k * SEQ) for k in blocks]
    tail_pos = [edge_pos((k + 1) * SEQ - _POOL_EDGE) for k in blocks]

    def patch(x, head_fn=None, tail_fn=None, body_fn=None):
        pieces = []
        for k in blocks:
            lo, hi = k * SEQ, (k + 1) * SEQ
            head, body, tail = x[lo:lo + _POOL_EDGE], x[lo + _POOL_EDGE:hi - _POOL_EDGE], x[hi - _POOL_EDGE:hi]
            pieces += [head_fn(head, k) if head_fn else head, body_fn(body) if body_fn else body,
                       tail_fn(tail, k) if tail_fn else tail]
        return jnp.concatenate(pieces, axis=0)

    def fwd(a, s):
        return patch(pltpu.roll(a, UNIT - s, 0), tail_fn=lambda x, k: jnp.where(tail_pos[k] < n - s, x, 0.0))

    def bwd(a, s):
        return patch(pltpu.roll(a, s, 0), head_fn=lambda x, k: jnp.where(head_pos[k] >= s, x, 0.0))

    for gi, w in enumerate(POOL_WINDOWS):
        half = w // 2
        cols = slice(gi * PG, (gi + 1) * PG)
        g = u_ref[:, cols]
        f, b, s = g, g, 1
        while s < half:
            f = f + fwd(f, s)
            b = b + bwd(b, s)
            s *= 2
        tot = f + bwd(b, 1)

        def edge_mean(x, pos):
            cnt = jnp.minimum(pos + half, n) - jnp.maximum(pos - half, 0)
            return x / cnt.astype(F32)

        mean = patch(tot, head_fn=lambda x, k: edge_mean(x, head_pos[k]),
                     tail_fn=lambda x, k: edge_mean(x, tail_pos[k]), body_fn=lambda x: x * (1.0 / w))
        dlt = mean - g
        out = _dot(dlt.astype(BF16), wp_ref[gi].astype(BF16)) * ps_ref[layer:layer + 1, cols]
        o_ref[:, cols] = out.astype(o_ref.dtype)


def _pool_mixer(u, w_pool, pool_scale, layer):
    return pl.pallas_call(
        functools.partial(_pool_kernel, layer=layer),
        grid=(N_UNITS,),
        in_specs=[pl.BlockSpec((UNIT, POOL_W), lambda i: (i, 0)),
                  pl.BlockSpec((None, len(POOL_WINDOWS), PG, PG), lambda i: (layer, 0, 0, 0)),
                  pl.BlockSpec((DEPTH, POOL_W), lambda i: (0, 0))],
        out_specs=pl.BlockSpec((UNIT, POOL_W), lambda i: (i, 0)),
        out_shape=jax.ShapeDtypeStruct((N_TOK, POOL_W), BF16),
        compiler_params=_cparams(("arbitrary",), 40),
        name="pool",
    )(u, w_pool, pool_scale)


_OUTPROJ_TILE = 512
_OUTPROJ_ROWS = 256


def _outproj_kernel(*refs, n_src, layer):
    x_refs = refs[:n_src]
    att_ref, pool_ref, w_ref, gate_ref, g2_ref, sh_ref, sc_ref, wr_ref, y_ref, h_ref, lg_ref = refs[n_src:]
    tm = _OUTPROJ_TILE
    for r in range(tm // _OUTPROJ_ROWS):
        rows = slice(r * _OUTPROJ_ROWS, (r + 1) * _OUTPROJ_ROWS)
        mix = _dot(att_ref[rows, :], w_ref[0:NA_W, :]) + _dot(pool_ref[rows, :], w_ref[NA_W:NA_W + POOL_W, :])
        y = _read_tokens(x_refs, tm, rows) + _mod_vec(gate_ref, tm) * mix
        y_ref[rows, :] = y
        h = _norm_mod(y, g2_ref[layer:layer + 1, :], _mod_vec(sh_ref, tm), _mod_vec(sc_ref, tm)).astype(BF16)
        h_ref[rows, :] = h
        lg_ref[rows, :] = _dot(h, wr_ref[...])


def _outproj(srcs, att, pool, mods, norm_gain, w_out_bf, w_router_pad, layer):
    tm = _OUTPROJ_TILE
    row = lambda width: pl.BlockSpec((tm, width), lambda i: (i, 0))
    return pl.pallas_call(
        functools.partial(_outproj_kernel, n_src=len(srcs), layer=layer),
        grid=(N_TOK // tm,),
        in_specs=_token_specs(srcs, tm, D) + [
            row(NA_W), row(POOL_W),
            _resident((D, D), lambda i: (0, 0)),
            _mod_spec(layer, 2), _gain_spec(), _mod_spec(layer, 3), _mod_spec(layer, 4),
            _resident((D, LANES), lambda i: (0, 0))],
        out_specs=[row(D), row(D), row(LANES)],
        out_shape=[jax.ShapeDtypeStruct((N_TOK, D), F32),
                   jax.ShapeDtypeStruct((N_TOK, D), BF16),
                   jax.ShapeDtypeStruct((N_TOK, LANES), F32)],
        compiler_params=_cparams(("arbitrary",), 52),
        name="outproj",
    )(*srcs, att, pool, w_out_bf, mods, norm_gain, mods, mods, w_router_pad)


_SEARCH_BITS = 3


def _choose_slots(aff, n_req, n_tok, cap):
    rows = jnp.concatenate([aff[:, r * n_tok:(r + 1) * n_tok] for r in range(n_req)], axis=0)
    n_rows = n_req * E
    thr = jnp.zeros((n_rows, 1), jnp.int32)
    hi = 31
    while hi > 0:
        lo = max(hi - _SEARCH_BITS, 0)
        n_cand = (1 << (hi - lo)) - 1
        hits = jnp.concatenate([(rows >= lax.bitcast_convert_type(thr | (j << lo), F32)).astype(F32)
                                for j in range(1, n_cand + 1)], axis=0)
        reached = (jnp.sum(hits, axis=-1, keepdims=True) >= cap).astype(jnp.int32)
        group = functools.reduce(jnp.add, [reached[j * n_rows:(j + 1) * n_rows] for j in range(n_cand)])
        thr = thr | lax.shift_left(group, jnp.int32(lo))
        hi = lo
    thr_val = lax.bitcast_convert_type(thr, F32)
    gt = rows > thr_val
    eq = rows == thr_val
    n_gt = jnp.sum(gt.astype(F32), axis=-1, keepdims=True)
    before = (lax.broadcasted_iota(jnp.int32, (n_tok, n_tok), 0)
              < lax.broadcasted_iota(jnp.int32, (n_tok, n_tok), 1)).astype(BF16)
    eq_rank = _dot(eq.astype(BF16), before)
    sel = gt | (eq & (eq_rank < cap - n_gt))
    slot = _dot(sel.astype(BF16), before)
    pieces = [jnp.where(sel[r * E:(r + 1) * E], slot[r * E:(r + 1) * E] + r * cap, -1.0) for r in range(n_req)]
    return jnp.concatenate(pieces, axis=1) if n_req > 1 else pieces[0]


def _route_kernel(lg_ref, pos_ref, aff_ref):
    lg = lg_ref[...].T[0:E, :]
    ex = jnp.exp(lg - jnp.max(lg, axis=0, keepdims=True))
    aff = ex / jnp.sum(ex, axis=0, keepdims=True)
    aff_ref[0] = aff

    @pl.when(pl.program_id(0) < CTX_UNITS)
    def _():
        pos_ref[0] = _choose_slots(aff, REQ_PER_CTX_UNIT, SEQ, CAP_CTX)

    @pl.when(pl.program_id(0) >= CTX_UNITS)
    def _():
        pos_ref[0] = _choose_slots(aff, 1, DEC_SEQ, CAP_LAT)


def _route(logits):
    out_blk = lambda: pl.BlockSpec((1, E, UNIT), lambda u: (u, 0, 0))
    return pl.pallas_call(
        _route_kernel,
        grid=(N_UNITS,),
        in_specs=[pl.BlockSpec((UNIT, LANES), lambda u: (u, 0))],
        out_specs=[out_blk(), out_blk()],
        out_shape=[jax.ShapeDtypeStruct((N_UNITS, E, UNIT), F32)] * 2,
        compiler_params=_cparams(("arbitrary",), 32),
        name="route",
    )(logits)


_COL_CHUNK = 512


def _onehot_rows(pos, p_ref, n_slots, first_slot):
    width = pos.shape[1]
    slot_id = (lax.broadcasted_iota(jnp.int32, (n_slots, width), 0) + first_slot).astype(F32)
    matches = []
    for e in range(E):
        match = pos[e:e + 1, :] == slot_id
        p_ref[e * n_slots:(e + 1) * n_slots, 0:width] = jnp.where(match, 1.0, 0.0).astype(BF16)
        matches.append(match)
    return matches


def _gather_kernel(pos_ref, aff_ref, h_ref, xs_ref, gs_ref, p_ref):
    def gather(tokens, slots, first_slot):
        n = slots.stop - slots.start
        aff = aff_ref[0, :, tokens]
        matches = _onehot_rows(pos_ref[0, :, tokens], p_ref, n, first_slot)
        for e in range(E):
            gate = jnp.sum(jnp.where(matches[e], aff[e:e + 1, :], 0.0), axis=-1, keepdims=True)
            gs_ref[e, slots, :] = jnp.broadcast_to(gate, (n, LANES))
        width = tokens.stop - tokens.start
        for c in range(D // _COL_CHUNK):
            cols = slice(c * _COL_CHUNK, (c + 1) * _COL_CHUNK)
            r = _dot(p_ref[0:E * n, 0:width], h_ref[tokens, cols])
            xs_ref[:, slots, cols] = r.reshape(E, n, _COL_CHUNK).astype(BF16)

    @pl.when(pl.program_id(0) < CTX_UNITS)
    def _():
        for r in range(REQ_PER_CTX_UNIT):
            gather(slice(r * SEQ, (r + 1) * SEQ), slice(r * CAP_CTX, (r + 1) * CAP_CTX), r * CAP_CTX)

    @pl.when(pl.program_id(0) >= CTX_UNITS)
    def _():
        gather(slice(0, UNIT), slice(0, SLOTS), 0)


def _gather(pos, aff, h):
    unit3 = lambda: pl.BlockSpec((1, E, UNIT), lambda u: (u, 0, 0))
    return pl.pallas_call(
        _gather_kernel,
        grid=(N_UNITS,),
        in_specs=[unit3(), unit3(), pl.BlockSpec((UNIT, D), lambda u: (u, 0))],
        out_specs=[pl.BlockSpec((E, SLOTS, D), lambda u: (0, u, 0)),
                   pl.BlockSpec((E, SLOTS, LANES), lambda u: (0, u, 0))],
        out_shape=[jax.ShapeDtypeStruct((E, N_SLOT, D), BF16),
                   jax.ShapeDtypeStruct((E, N_SLOT, LANES), F32)],
        scratch_shapes=[pltpu.VMEM((E * SLOTS, UNIT), BF16)],
        compiler_params=_cparams(("arbitrary",), 48),
        name="gather",
    )(pos, aff, h)


_FFN_ROWS = 256
_FFN_COLS = 256


def _ffn_up_kernel(xs_ref, wg_ref, wu_ref, o_ref):
    for c in range(wg_ref.shape[1] // _FFN_COLS):
        cols = slice(c * _FFN_COLS, (c + 1) * _FFN_COLS)
        wg = wg_ref[:, cols].astype(BF16)
        wu = wu_ref[:, cols].astype(BF16)
        for m in range(N_SLOT // _FFN_ROWS):
            rows = slice(m * _FFN_ROWS, (m + 1) * _FFN_ROWS)
            x = xs_ref[rows, :]
            o_ref[rows, cols] = (_silu(_dot(x, wg)) * _dot(x, wu)).astype(o_ref.dtype)


def _ffn_up(xs, w_gate, w_up, layer):
    tf = 512
    wspec = lambda: pl.BlockSpec((None, None, D, tf), lambda e, j: (layer, e, 0, j))
    return pl.pallas_call(
        _ffn_up_kernel,
        grid=(E, D_EXP // tf),
        in_specs=[pl.BlockSpec((None, N_SLOT, D), lambda e, j: (e, 0, 0)), wspec(), wspec()],
        out_specs=pl.BlockSpec((None, N_SLOT, tf), lambda e, j: (e, 0, j)),
        out_shape=jax.ShapeDtypeStruct((E, N_SLOT, D_EXP), BF16),
        compiler_params=_cparams(("arbitrary", "arbitrary"), 48),
        name="ffn_up",
    )(xs, w_gate, w_up)


def _ffn_down_kernel(h_ref, wd_ref, gs_ref, o_ref):
    for c in range(wd_ref.shape[1] // _FFN_COLS):
        cols = slice(c * _FFN_COLS, (c + 1) * _FFN_COLS)
        wd = wd_ref[:, cols].astype(BF16)
        for m in range(N_SLOT // _FFN_ROWS):
            rows = slice(m * _FFN_ROWS, (m + 1) * _FFN_ROWS)
            gate = jnp.concatenate([gs_ref[rows, :]] * (_FFN_COLS // LANES), axis=1)
            o_ref[rows, cols] = (_dot(h_ref[rows, :], wd) * gate).astype(o_ref.dtype)


def _ffn_down(hcur, w_down, gslot, layer):
    return pl.pallas_call(
        _ffn_down_kernel,
        grid=(E,),
        in_specs=[pl.BlockSpec((None, N_SLOT, D_EXP), lambda e: (e, 0, 0)),
                  pl.BlockSpec((None, None, D_EXP, D), lambda e: (layer, e, 0, 0)),
                  pl.BlockSpec((None, N_SLOT, LANES), lambda e: (e, 0, 0))],
        out_specs=pl.BlockSpec((None, N_SLOT, D), lambda e: (e, 0, 0)),
        out_shape=jax.ShapeDtypeStruct((E, N_SLOT, D), BF16),
        compiler_params=_cparams(("arbitrary",), 48),
        name="ffn_down",
    )(hcur, w_down, gslot)


_SC_TOK = 512


def _scatter_kernel(pos_ref, ys_ref, y_ref, gate_ref, *rest, final):
    if final:
        gain_ref, oc_ref, ol_ref, p_ref = rest
    else:
        o_ref, p_ref = rest

    def emit(rows, updates, is_ctx):
        y = y_ref[rows, :] + _mod_vec(gate_ref, UNIT) * jnp.concatenate(updates, axis=1)
        if not final:
            o_ref[rows, :] = y
        else:
            r = y * lax.rsqrt(jnp.mean(y * y, axis=-1, keepdims=True) + RMS_EPS) * gain_ref[...]
            (oc_ref if is_ctx else ol_ref)[rows, :] = r

    col_chunks = [slice(c * _COL_CHUNK, (c + 1) * _COL_CHUNK) for c in range(D // _COL_CHUNK)]

    @pl.when(pl.program_id(0) < CTX_UNITS)
    def _():
        for r in range(_SC_TOK // SEQ):
            rows = slice(r * SEQ, (r + 1) * SEQ)
            first = pl.multiple_of((pl.program_id(1) * (_SC_TOK // SEQ) + r) * CAP_CTX, CAP_CTX)
            p_rows = p_ref.at[r * E * CAP_CTX:(r + 1) * E * CAP_CTX]
            _onehot_rows(pos_ref[0, :, rows], p_rows, CAP_CTX, first)
            emit(rows, [_dot_tn(p_rows[:, 0:SEQ],
                                ys_ref[:, pl.ds(first, CAP_CTX), cols].reshape(E * CAP_CTX, _COL_CHUNK))
                        for cols in col_chunks], True)

    @pl.when(pl.program_id(0) >= CTX_UNITS)
    def _():
        _onehot_rows(pos_ref[0], p_ref, SLOTS, 0)
        emit(slice(0, _SC_TOK), [_dot_tn(p_ref[...], ys_ref[:, :, cols].reshape(E * SLOTS, _COL_CHUNK))
                                 for cols in col_chunks], False)


def _scatter(pos, ys, y, mods, layer, final_gain=None):
    assert _SC_TOK % SEQ == 0 and UNIT % _SC_TOK == 0
    per_unit = UNIT // _SC_TOK
    n_ctx_tiles = N_CTX // _SC_TOK
    final = final_gain is not None
    tok = lambda: pl.BlockSpec((_SC_TOK, D), lambda u, s: (u * per_unit + s, 0))
    in_specs = [pl.BlockSpec((1, E, _SC_TOK), lambda u, s: (u, 0, s)),
                pl.BlockSpec((E, SLOTS, D), lambda u, s: (0, u, 0)),
                tok(),
                _mod_spec(layer, 5)]
    args = [pos, ys, y, mods]
    if final:
        in_specs.append(pl.BlockSpec((1, D), lambda u, s: (0, 0)))
        args.append(final_gain.reshape(1, D))
        out_specs = [pl.BlockSpec((_SC_TOK, D), lambda u, s: (jnp.minimum(u * per_unit + s, n_ctx_tiles - 1), 0)),
                     pl.BlockSpec((_SC_TOK, D), lambda u, s: (jnp.maximum(u * per_unit + s - n_ctx_tiles, 0), 0))]
        out_shape = [jax.ShapeDtypeStruct((N_CTX, D), F32), jax.ShapeDtypeStruct((N_LAT, D), F32)]
    else:
        out_specs = tok()
        out_shape = jax.ShapeDtypeStruct((N_TOK, D), F32)
    return pl.pallas_call(
        functools.partial(_scatter_kernel, final=final),
        grid=(N_UNITS, per_unit),
        in_specs=in_specs,
        out_specs=out_specs,
        out_shape=out_shape,
        scratch_shapes=[pltpu.VMEM((E * SLOTS, _SC_TOK), BF16)],
        compiler_params=_cparams(("arbitrary", "arbitrary"), 56),
        name="scatter",
    )(*args)


def kernel(x_prompt, x_sample, cache_k, cache_v, c, c_ctx, w_mod, b_mod, norm_mix, norm_ffn, w_in, rpb,
           w_pool, pool_scale, w_out, w_router, w_gate, w_up, w_down, norm_final):
    srcs = [x_prompt.reshape(N_CTX, D), x_sample.reshape(N_LAT, D)]
    cond = jnp.zeros((N_ROW, D), F32).at[0].set(c_ctx).at[1:1 + DEC_BATCH].set(c)
    mods = _adaln(cond, w_mod, b_mod)
    ck = cache_k.reshape(DEC_BATCH, DEPTH, PAST, NA_W).astype(BF16)
    cv = cache_v.reshape(DEC_BATCH, DEPTH, PAST, NA_W).astype(BF16)

    new_kv = []
    for l in range(DEPTH):
        q, k, v, u, *new_kv = _inproj(srcs, mods, norm_mix, _to_bf16(w_in, l), l, new_kv)
        att = _attention(q, k, v, ck, cv, rpb, l)
        pool = _pool_mixer(u, w_pool, pool_scale, l)

        w_router_pad = jnp.pad(w_router[l], ((0, 0), (0, LANES - E))).astype(BF16)
        y, h, logits = _outproj(srcs, att, pool, mods, norm_ffn, _to_bf16(w_out, l), w_router_pad, l)

        pos, aff = _route(logits)
        xs, gslot = _gather(pos, aff, h)
        ys = _ffn_down(_ffn_up(xs, w_gate, w_up, l), w_down, gslot, l)
        srcs = _scatter(pos, ys, y, mods, l, norm_final if l == DEPTH - 1 else None)
        srcs = list(srcs) if l == DEPTH - 1 else [srcs]

    y_prompt, y_sample = srcs
    new_k, new_v = new_kv
    return (y_prompt.reshape(BATCH, SEQ, D), y_sample.reshape(DEC_BATCH, DEC_SEQ, D),
            new_k.reshape(BATCH, DEPTH, SEQ, H, DH), new_v.reshape(BATCH, DEPTH, SEQ, H, DH))
```

```python
import functools

import jax
import jax.numpy as jnp
from jax import lax
from jax.experimental import pallas as pl
from jax.experimental.pallas import tpu as pltpu

F32 = jnp.float32
BF16 = jnp.bfloat16

D = 2048
BATCH, SEQ = 16, 256
DEPTH = 2
DEC_BATCH, DEC_SEQ = 8, 1024
PAST = 512
GRID_W = 64
GRID_H = DEC_SEQ // GRID_W
H, DH = 16, 64
NA_W = H * DH
POOL_WINDOWS = (2, 4, 8, 16)
PG = 256
POOL_W = 1024
IN_W = 3 * NA_W + POOL_W
NA_ROWS, NA_COLS = 8, 16
E = 16
D_EXP = 1024
N_MOD = 6
RMS_EPS = 1e-6
NEG_INF = -1e30

N_CTX = BATCH * SEQ
N_LAT = DEC_BATCH * DEC_SEQ
N_TOK = N_CTX + N_LAT
UNIT = 1024
N_UNITS = N_TOK // UNIT
CTX_UNITS = N_CTX // UNIT
REQ_PER_CTX_UNIT = UNIT // SEQ
SLOTS = 128
CAP_CTX = 2 * SEQ // E
CAP_LAT = 2 * DEC_SEQ // E
N_SLOT = N_UNITS * SLOTS
N_ROW = 16
LANES = 128
MIB = 1024 * 1024


def _cparams(sem, vmem_mib):
    return pltpu.CompilerParams(dimension_semantics=sem, vmem_limit_bytes=vmem_mib * MIB)


def _resident(block_shape, index_map):
    return pl.BlockSpec(block_shape, index_map, pipeline_mode=pl.Buffered(1))


def _dot(a, b):
    return jnp.dot(a, b, preferred_element_type=F32)


def _dot_nt(a, b):
    return lax.dot_general(a, b, (((1,), (1,)), ((), ())), preferred_element_type=F32)


def _dot_tn(a, b):
    return lax.dot_general(a, b, (((0,), (0,)), ((), ())), preferred_element_type=F32)


def _silu(x):
    return x / (1.0 + jnp.exp(-x))


def _norm_mod(x, gain, shift, scale):
    y = x * lax.rsqrt(jnp.mean(x * x, axis=-1, keepdims=True) + RMS_EPS)
    return (y * gain) * (1.0 + scale) + shift


def _mod_row_of_tile(i, tile):
    per_req = DEC_SEQ // tile
    n_ctx_tiles = N_CTX // tile
    return jnp.where(i < n_ctx_tiles, 0, 1 + (i - n_ctx_tiles) // per_req)


def _mod_spec(layer, which):
    return pl.BlockSpec((None, N_ROW, D), lambda *ids: (layer, 0, which))


def _mod_vec(ref, tile, axis=0):
    return ref[pl.ds(_mod_row_of_tile(pl.program_id(axis), tile), 1), :]


def _gain_spec():
    return pl.BlockSpec((DEPTH, D), lambda *ids: (0, 0))


def _token_specs(srcs, tile, width):
    if len(srcs) == 1:
        return [pl.BlockSpec((tile, width), lambda i: (i, 0))]
    n_ctx_tiles = N_CTX // tile
    return [pl.BlockSpec((tile, width), lambda i: (jnp.minimum(i, n_ctx_tiles - 1), 0)),
            pl.BlockSpec((tile, width), lambda i: (jnp.maximum(i - n_ctx_tiles, 0), 0))]


def _read_tokens(refs, tile, rows=slice(None)):
    if len(refs) == 1:
        return refs[0][rows, :]
    return jnp.where(pl.program_id(0) < N_CTX // tile, refs[0][rows, :], refs[1][rows, :])


def _adaln_kernel(c_ref, w_ref, b_ref, o_ref):
    a = _silu(c_ref[...]).astype(BF16)
    o_ref[0] = _dot(a, w_ref[0].astype(BF16)) + b_ref[pl.ds(pl.program_id(0), 1), :]


def _adaln(cond, w_mod, b_mod):
    tn = 1024
    return pl.pallas_call(
        _adaln_kernel,
        grid=(DEPTH, N_MOD * D // tn),
        in_specs=[pl.BlockSpec((N_ROW, D), lambda l, j: (0, 0)),
                  pl.BlockSpec((1, D, tn), lambda l, j: (l, 0, j)),
                  pl.BlockSpec((DEPTH, tn), lambda l, j: (0, j))],
        out_specs=pl.BlockSpec((1, N_ROW, tn), lambda l, j: (l, 0, j)),
        out_shape=jax.ShapeDtypeStruct((DEPTH, N_ROW, N_MOD * D), F32),
        compiler_params=_cparams(("arbitrary", "arbitrary"), 40),
        name="adaln",
    )(cond, w_mod, b_mod)


def _cast_kernel(w_ref, o_ref):
    o_ref[...] = w_ref[...].astype(BF16)


def _to_bf16(w, layer):
    rows, cols = w.shape[1:]
    tr = 512
    return pl.pallas_call(
        _cast_kernel,
        grid=(rows // tr,),
        in_specs=[pl.BlockSpec((None, tr, cols), lambda i: (layer, i, 0))],
        out_specs=pl.BlockSpec((tr, cols), lambda i: (i, 0)),
        out_shape=jax.ShapeDtypeStruct((rows, cols), BF16),
        compiler_params=_cparams(("arbitrary",), 32),
        name="cast_bf16",
    )(w)


_INPROJ_TILE = 256


def _inproj_kernel(*refs, n_src, n_prev, layer, tm):
    x_refs = refs[:n_src]
    g_ref, sh_ref, sc_ref, w_ref = refs[n_src:n_src + 4]
    q_ref, k_ref, v_ref, u_ref, kf_ref, vf_ref = refs[n_src + 4 + n_prev:]
    for r in range(tm // SEQ):
        rows = slice(r * SEQ, (r + 1) * SEQ)
        h = _norm_mod(_read_tokens(x_refs, tm, rows), g_ref[layer:layer + 1, :], _mod_vec(sh_ref, tm),
                      _mod_vec(sc_ref, tm)).astype(BF16)
        q_ref[rows, :] = (_dot(h, w_ref[:, 0:NA_W]) * (DH ** -0.5)).astype(BF16)
        k = _dot(h, w_ref[:, NA_W:2 * NA_W])
        v = _dot(h, w_ref[:, 2 * NA_W:3 * NA_W])
        k_ref[rows, :] = k.astype(BF16)
        v_ref[rows, :] = v.astype(BF16)
        u_ref[rows, :] = _dot(h, w_ref[:, 3 * NA_W:IN_W])

        @pl.when(pl.program_id(0) < N_CTX // tm)
        def _():
            for slab in range(kf_ref.shape[1]):
                mine = slab == (layer if n_prev == 0 else 0)
                kf_ref[r, slab] = k if mine else jnp.zeros_like(k)
                vf_ref[r, slab] = v if mine else jnp.zeros_like(v)


def _inproj(srcs, mods, norm_gain, w_in_bf, layer, kv_prev):
    tm = _INPROJ_TILE if len(srcs) > 1 else 2 * _INPROJ_TILE
    n_ctx_tiles = N_CTX // tm
    tok = lambda: pl.BlockSpec((tm, NA_W), lambda i: (i, 0))
    if kv_prev:
        ctx_only = lambda: pl.BlockSpec((tm // SEQ, 1, SEQ, NA_W),
                                        lambda i: (jnp.minimum(i, n_ctx_tiles - 1), layer, 0, 0))
    else:
        ctx_only = lambda: pl.BlockSpec((tm // SEQ, DEPTH, SEQ, NA_W),
                                        lambda i: (jnp.minimum(i, n_ctx_tiles - 1), 0, 0, 0))
    n_in = len(srcs) + 4
    return pl.pallas_call(
        functools.partial(_inproj_kernel, n_src=len(srcs), n_prev=len(kv_prev), layer=layer, tm=tm),
        grid=(N_TOK // tm,),
        in_specs=_token_specs(srcs, tm, D) + [
            _gain_spec(), _mod_spec(layer, 0), _mod_spec(layer, 1),
            _resident((D, IN_W), lambda i: (0, 0))] + [pl.BlockSpec(memory_space=pl.ANY)] * len(kv_prev),
        out_specs=[tok(), tok(), tok(), tok(), ctx_only(), ctx_only()],
        out_shape=[jax.ShapeDtypeStruct((N_TOK, NA_W), BF16)] * 3
        + [jax.ShapeDtypeStruct((N_TOK, POOL_W), F32)]
        + [jax.ShapeDtypeStruct((BATCH, DEPTH, SEQ, NA_W), F32)] * 2,
        input_output_aliases={n_in + j: 4 + j for j in range(len(kv_prev))},
        compiler_params=_cparams(("arbitrary",), 56),
        name="inproj",
    )(*srcs, norm_gain, mods, mods, w_in_bf, *kv_prev)


def _na_window_start(qr):
    return min(max(qr - NA_ROWS // 2, 0), GRID_H - NA_ROWS)


def _fill_na_bias(rpb_ref, o_ref):
    assert LANES == 2 * GRID_W
    qc = lax.broadcasted_iota(jnp.int32, (GRID_W, LANES), 0)
    lane = lax.broadcasted_iota(jnp.int32, (GRID_W, LANES), 1)
    kc = jnp.bitwise_and(lane, GRID_W - 1)
    cs = jnp.clip(qc - NA_COLS // 2, 0, GRID_W - NA_COLS)
    in_col = (kc >= cs) & (kc < cs + NA_COLS)
    keep_lanes = {(True, True): in_col, (True, False): in_col & (lane < GRID_W),
                  (False, True): in_col & (lane >= GRID_W)}
    n_dr = 2 * NA_ROWS - 1
    zero_rows = jnp.zeros((GRID_W, LANES), F32)

    def rpb_rows(dr):
        if not 0 <= dr < n_dr:
            return zero_rows
        return jnp.broadcast_to(rpb_ref[dr:dr + 1, :], (GRID_W, LANES))

    pair_cache = {}

    def pair_tile(dr, use_first, use_second):
        if not (use_first or use_second):
            return jnp.full((GRID_W, LANES), NEG_INF, F32)
        key = (dr, use_first, use_second)
        if key not in pair_cache:
            src = rpb_rows(dr) + pltpu.roll(rpb_rows(dr + 1), GRID_W, 1)
            toep = pltpu.roll(src, LANES - (NA_COLS - 1), 1, stride=1, stride_axis=0)
            pair_cache[key] = jnp.where(keep_lanes[(use_first, use_second)], toep, NEG_INF)
        return pair_cache[key]

    for qr in range(GRID_H):
        rs = _na_window_start(qr)
        for kr in range(0, GRID_H, 2):
            use = [rs <= k < rs + NA_ROWS for k in (kr, kr + 1)]
            o_ref[qr * GRID_W:(qr + 1) * GRID_W, kr * GRID_W:(kr + 2) * GRID_W] = pair_tile(
                kr - qr + NA_ROWS - 1, use[0], use[1])


_NA_Q_BLOCK = 4 * GRID_W


def _na_key_range(qb):
    rows_per_block = _NA_Q_BLOCK // GRID_W
    lo = _na_window_start(qb * rows_per_block) * GRID_W
    hi = (_na_window_start((qb + 1) * rows_per_block - 1) + NA_ROWS) * GRID_W
    return (lo // _NA_Q_BLOCK * _NA_Q_BLOCK, -(-hi // _NA_Q_BLOCK) * _NA_Q_BLOCK)


_NA_KEY_RANGES = [_na_key_range(qb) for qb in range(DEC_SEQ // _NA_Q_BLOCK)]


_PAIRS_PER_STEP = 4


def _attn_kernel(q_ref, k_ref, v_ref, ck_ref, cv_ref, rpb_ref, o_ref, kc_ref, vv_ref, tab_ref):
    @pl.when(pl.program_id(1) == CTX_UNITS)
    def _():
        for h in range(2 * _PAIRS_PER_STEP):
            _fill_na_bias(rpb_ref.at[h], tab_ref.at[h])

    lane = lax.broadcasted_iota(jnp.int32, (1, 2 * DH), 1)
    head_lanes = [lane < DH, lane >= DH]

    def probs(scores):
        m = functools.reduce(jnp.maximum, [jnp.max(s, axis=-1, keepdims=True) for s in scores])
        return [jnp.exp(s - m).astype(BF16) for s in scores]

    def merge_heads(acc):
        outs = [a[:, 0:2 * DH] / a[:, 2 * DH:4 * DH] for a in acc]
        return jnp.where(head_lanes[0], outs[0], outs[1]).astype(o_ref.dtype)

    pairs = [(pp, slice(pp * 2 * DH, (pp + 1) * 2 * DH)) for pp in range(_PAIRS_PER_STEP)]
    for pp, pair in pairs:
        vv_ref[pp, 0:UNIT, 0:2 * DH] = v_ref[:, pair]
        vv_ref[pp, :, 2 * DH:4 * DH] = jnp.ones((UNIT + PAST, 2 * DH), BF16)

    def head_queries(rows, pair):
        q = q_ref[rows, pair]
        return [jnp.where(m, q, jnp.zeros_like(q)) for m in head_lanes]

    @pl.when(pl.program_id(1) < CTX_UNITS)
    def _():
        for pp, pair in pairs:
            for r in range(REQ_PER_CTX_UNIT):
                rows = slice(r * SEQ, (r + 1) * SEQ)
                acc = []
                for qh in head_queries(rows, pair):
                    (p,) = probs([_dot_nt(qh, k_ref[rows, pair])])
                    acc.append(_dot(p, vv_ref[pp, rows, :]))
                o_ref[rows, pair] = merge_heads(acc)

    @pl.when(pl.program_id(1) >= CTX_UNITS)
    def _():
        for pp, pair in pairs:
            kc_ref[pp] = ck_ref[0, 0, :, pair].astype(BF16)
            vv_ref[pp, UNIT:UNIT + PAST, 0:2 * DH] = cv_ref[0, 0, :, pair].astype(BF16)
            for qb, (k_lo, k_hi) in enumerate(_NA_KEY_RANGES):
                rows = slice(qb * _NA_Q_BLOCK, (qb + 1) * _NA_Q_BLOCK)
                keys = slice(k_lo, k_hi)
                acc = []
                for hh, qh in enumerate(head_queries(rows, pair)):
                    s_lat = _dot_nt(qh, k_ref[keys, pair]) + tab_ref[2 * pp + hh, rows, keys]
                    s_ctx = _dot_nt(qh, kc_ref[pp])
                    p_lat, p_ctx = probs([s_lat, s_ctx])
                    acc.append(_dot(p_lat, vv_ref[pp, keys, :]) + _dot(p_ctx, vv_ref[pp, UNIT:UNIT + PAST, :]))
                o_ref[rows, pair] = merge_heads(acc)


def _attention(q, k, v, cache_k, cache_v, rpb, layer):
    width = _PAIRS_PER_STEP * 2 * DH
    blk = lambda: pl.BlockSpec((UNIT, width), lambda g, u: (u, g))
    cache = lambda: pl.BlockSpec((1, 1, PAST, width), lambda g, u: (jnp.maximum(u - CTX_UNITS, 0), layer, 0, g))
    rpb_pad = jnp.pad(rpb, ((0, 0), (0, 0), (0, 0), (0, LANES - rpb.shape[-1])))
    return pl.pallas_call(
        _attn_kernel,
        grid=(NA_W // width, N_UNITS),
        in_specs=[blk(), blk(), blk(), cache(), cache(),
                  pl.BlockSpec((None, 2 * _PAIRS_PER_STEP, rpb.shape[2], LANES), lambda g, u: (layer, g, 0, 0))],
        out_specs=blk(),
        out_shape=jax.ShapeDtypeStruct((N_TOK, NA_W), BF16),
        scratch_shapes=[pltpu.VMEM((_PAIRS_PER_STEP, PAST, 2 * DH), BF16),
                        pltpu.VMEM((_PAIRS_PER_STEP, UNIT + PAST, 4 * DH), BF16),
                        pltpu.VMEM((2 * _PAIRS_PER_STEP, DEC_SEQ, DEC_SEQ), F32)],
        compiler_params=_cparams(("arbitrary", "arbitrary"), 56),
        name="attention",
    )(q, k, v, cache_k, cache_v, rpb_pad)


_POOL_EDGE = 8


def _pool_kernel(u_ref, wp_ref, ps_ref, o_ref, *, layer):
    n = jnp.where(pl.program_id(0) < CTX_UNITS, SEQ, DEC_SEQ)
    assert max(POOL_WINDOWS) // 2 <= _POOL_EDGE
    blocks = range(UNIT // SEQ)

    def edge_pos(first_row):
        t = lax.broadcasted_iota(jnp.int32, (_POOL_EDGE, PG), 0) + first_row
        return jnp.bitwise_and(t, n - 1)

    head_pos = [edge_pos(k * SEQ) for k in blocks]
    tail_pos = [edge_pos((k + 1) * SEQ - _POOL_EDGE) for k in blocks]

    def patch(x, head_fn=None, tail_fn=None, body_fn=None):
        pieces = []
        for k in blocks:
            lo, hi = k * SEQ, (k + 1) * SEQ
            head, body, tail = x[lo:lo + _POOL_EDGE], x[lo + _POOL_EDGE:hi - _POOL_EDGE], x[hi - _POOL_EDGE:hi]
            pieces += [head_fn(head, k) if head_fn else head, body_fn(body) if body_fn else body,
                       tail_fn(tail, k) if tail_fn else tail]
        return jnp.concatenate(pieces, axis=0)

    def fwd(a, s):
        return patch(pltpu.roll(a, UNIT - s, 0), tail_fn=lambda x, k: jnp.where(tail_pos[k] < n - s, x, 0.0))

    def bwd(a, s):
        return patch(pltpu.roll(a, s, 0), head_fn=lambda x, k: jnp.where(head_pos[k] >= s, x, 0.0))

    for gi, w in enumerate(POOL_WINDOWS):
        half = w // 2
        cols = slice(gi * PG, (gi + 1) * PG)
        g = u_ref[:, cols]
        f, b, s = g, g, 1
        while s < half:
            f = f + fwd(f, s)
            b = b + bwd(b, s)
            s *= 2
        tot = f + bwd(b, 1)

        def edge_mean(x, pos):
            cnt = jnp.minimum(pos + half, n) - jnp.maximum(pos - half, 0)
            return x / cnt.astype(F32)

        mean = patch(tot, head_fn=lambda x, k: edge_mean(x, head_pos[k]),
                     tail_fn=lambda x, k: edge_mean(x, tail_pos[k]), body_fn=lambda x: x * (1.0 / w))
        dlt = mean - g
        out = _dot(dlt.astype(BF16), wp_ref[gi].astype(BF16)) * ps_ref[layer:layer + 1, cols]
        o_ref[:, cols] = out.astype(o_ref.dtype)


def _pool_mixer(u, w_pool, pool_scale, layer):
    return pl.pallas_call(
        functools.partial(_pool_kernel, layer=layer),
        grid=(N_UNITS,),
        in_specs=[pl.BlockSpec((UNIT, POOL_W), lambda i: (i, 0)),
                  pl.BlockSpec((None, len(POOL_WINDOWS), PG, PG), lambda i: (layer, 0, 0, 0)),
                  pl.BlockSpec((DEPTH, POOL_W), lambda i: (0, 0))],
        out_specs=pl.BlockSpec((UNIT, POOL_W), lambda i: (i, 0)),
        out_shape=jax.ShapeDtypeStruct((N_TOK, POOL_W), BF16),
        compiler_params=_cparams(("arbitrary",), 40),
        name="pool",
    )(u, w_pool, pool_scale)


_OUTPROJ_TILE = 512
_OUTPROJ_ROWS = 256


def _outproj_kernel(*refs, n_src, layer):
    x_refs = refs[:n_src]
    att_ref, pool_ref, w_ref, gate_ref, g2_ref, sh_ref, sc_ref, wr_ref, y_ref, h_ref, lg_ref = refs[n_src:]
    tm = _OUTPROJ_TILE
    for r in range(tm // _OUTPROJ_ROWS):
        rows = slice(r * _OUTPROJ_ROWS, (r + 1) * _OUTPROJ_ROWS)
        mix = _dot(att_ref[rows, :], w_ref[0:NA_W, :]) + _dot(pool_ref[rows, :], w_ref[NA_W:NA_W + POOL_W, :])
        y = _read_tokens(x_refs, tm, rows) + _mod_vec(gate_ref, tm) * mix
        y_ref[rows, :] = y
        h = _norm_mod(y, g2_ref[layer:layer + 1, :], _mod_vec(sh_ref, tm), _mod_vec(sc_ref, tm)).astype(BF16)
        h_ref[rows, :] = h
        lg_ref[rows, :] = _dot(h, wr_ref[...])


def _outproj(srcs, att, pool, mods, norm_gain, w_out_bf, w_router_pad, layer):
    tm = _OUTPROJ_TILE
    row = lambda width: pl.BlockSpec((tm, width), lambda i: (i, 0))
    return pl.pallas_call(
        functools.partial(_outproj_kernel, n_src=len(srcs), layer=layer),
        grid=(N_TOK // tm,),
        in_specs=_token_specs(srcs, tm, D) + [
            row(NA_W), row(POOL_W),
            _resident((D, D), lambda i: (0, 0)),
            _mod_spec(layer, 2), _gain_spec(), _mod_spec(layer, 3), _mod_spec(layer, 4),
            _resident((D, LANES), lambda i: (0, 0))],
        out_specs=[row(D), row(D), row(LANES)],
        out_shape=[jax.ShapeDtypeStruct((N_TOK, D), F32),
                   jax.ShapeDtypeStruct((N_TOK, D), BF16),
                   jax.ShapeDtypeStruct((N_TOK, LANES), F32)],
        compiler_params=_cparams(("arbitrary",), 52),
        name="outproj",
    )(*srcs, att, pool, w_out_bf, mods, norm_gain, mods, mods, w_router_pad)


_SEARCH_BITS = 3


def _choose_slots(aff, n_req, n_tok, cap):
    rows = jnp.concatenate([aff[:, r * n_tok:(r + 1) * n_tok] for r in range(n_req)], axis=0)
    n_rows = n_req * E
    thr = jnp.zeros((n_rows, 1), jnp.int32)
    hi = 31
    while hi > 0:
        lo = max(hi - _SEARCH_BITS, 0)
        n_cand = (1 << (hi - lo)) - 1
        hits = jnp.concatenate([(rows >= lax.bitcast_convert_type(thr | (j << lo), F32)).astype(F32)
                                for j in range(1, n_cand + 1)], axis=0)
        reached = (jnp.sum(hits, axis=-1, keepdims=True) >= cap).astype(jnp.int32)
        group = functools.reduce(jnp.add, [reached[j * n_rows:(j + 1) * n_rows] for j in range(n_cand)])
        thr = thr | lax.shift_left(group, jnp.int32(lo))
        hi = lo
    thr_val = lax.bitcast_convert_type(thr, F32)
    gt = rows > thr_val
    eq = rows == thr_val
    n_gt = jnp.sum(gt.astype(F32), axis=-1, keepdims=True)
    before = (lax.broadcasted_iota(jnp.int32, (n_tok, n_tok), 0)
              < lax.broadcasted_iota(jnp.int32, (n_tok, n_tok), 1)).astype(BF16)
    eq_rank = _dot(eq.astype(BF16), before)
    sel = gt | (eq & (eq_rank < cap - n_gt))
    slot = _dot(sel.astype(BF16), before)
    pieces = [jnp.where(sel[r * E:(r + 1) * E], slot[r * E:(r + 1) * E] + r * cap, -1.0) for r in range(n_req)]
    return jnp.concatenate(pieces, axis=1) if n_req > 1 else pieces[0]


def _route_kernel(lg_ref, pos_ref, aff_ref):
    lg = lg_ref[...].T[0:E, :]
    ex = jnp.exp(lg - jnp.max(lg, axis=0, keepdims=True))
    aff = ex / jnp.sum(ex, axis=0, keepdims=True)
    aff_ref[0] = aff

    @pl.when(pl.program_id(0) < CTX_UNITS)
    def _():
        pos_ref[0] = _choose_slots(aff, REQ_PER_CTX_UNIT, SEQ, CAP_CTX)

    @pl.when(pl.program_id(0) >= CTX_UNITS)
    def _():
        pos_ref[0] = _choose_slots(aff, 1, DEC_SEQ, CAP_LAT)


def _route(logits):
    out_blk = lambda: pl.BlockSpec((1, E, UNIT), lambda u: (u, 0, 0))
    return pl.pallas_call(
        _route_kernel,
        grid=(N_UNITS,),
        in_specs=[pl.BlockSpec((UNIT, LANES), lambda u: (u, 0))],
        out_specs=[out_blk(), out_blk()],
        out_shape=[jax.ShapeDtypeStruct((N_UNITS, E, UNIT), F32)] * 2,
        compiler_params=_cparams(("arbitrary",), 32),
        name="route",
    )(logits)


_COL_CHUNK = 512


def _onehot_rows(pos, p_ref, n_slots, first_slot):
    width = pos.shape[1]
    slot_id = (lax.broadcasted_iota(jnp.int32, (n_slots, width), 0) + first_slot).astype(F32)
    matches = []
    for e in range(E):
        match = pos[e:e + 1, :] == slot_id
        p_ref[e * n_slots:(e + 1) * n_slots, 0:width] = jnp.where(match, 1.0, 0.0).astype(BF16)
        matches.append(match)
    return matches


def _gather_kernel(pos_ref, aff_ref, h_ref, xs_ref, gs_ref, p_ref):
    def gather(tokens, slots, first_slot):
        n = slots.stop - slots.start
        aff = aff_ref[0, :, tokens]
        matches = _onehot_rows(pos_ref[0, :, tokens], p_ref, n, first_slot)
        for e in range(E):
            gate = jnp.sum(jnp.where(matches[e], aff[e:e + 1, :], 0.0), axis=-1, keepdims=True)
            gs_ref[e, slots, :] = jnp.broadcast_to(gate, (n, LANES))
        width = tokens.stop - tokens.start
        for c in range(D // _COL_CHUNK):
            cols = slice(c * _COL_CHUNK, (c + 1) * _COL_CHUNK)
            r = _dot(p_ref[0:E * n, 0:width], h_ref[tokens, cols])
            xs_ref[:, slots, cols] = r.reshape(E, n, _COL_CHUNK).astype(BF16)

    @pl.when(pl.program_id(0) < CTX_UNITS)
    def _():
        for r in range(REQ_PER_CTX_UNIT):
            gather(slice(r * SEQ, (r + 1) * SEQ), slice(r * CAP_CTX, (r + 1) * CAP_CTX), r * CAP_CTX)

    @pl.when(pl.program_id(0) >= CTX_UNITS)
    def _():
        gather(slice(0, UNIT), slice(0, SLOTS), 0)


def _gather(pos, aff, h):
    unit3 = lambda: pl.BlockSpec((1, E, UNIT), lambda u: (u, 0, 0))
    return pl.pallas_call(
        _gather_kernel,
        grid=(N_UNITS,),
        in_specs=[unit3(), unit3(), pl.BlockSpec((UNIT, D), lambda u: (u, 0))],
        out_specs=[pl.BlockSpec((E, SLOTS, D), lambda u: (0, u, 0)),
                   pl.BlockSpec((E, SLOTS, LANES), lambda u: (0, u, 0))],
        out_shape=[jax.ShapeDtypeStruct((E, N_SLOT, D), BF16),
                   jax.ShapeDtypeStruct((E, N_SLOT, LANES), F32)],
        scratch_shapes=[pltpu.VMEM((E * SLOTS, UNIT), BF16)],
        compiler_params=_cparams(("arbitrary",), 48),
        name="gather",
    )(pos, aff, h)


_FFN_ROWS = 256
_FFN_COLS = 256


def _ffn_up_kernel(xs_ref, wg_ref, wu_ref, o_ref):
    for c in range(wg_ref.shape[1] // _FFN_COLS):
        cols = slice(c * _FFN_COLS, (c + 1) * _FFN_COLS)
        wg = wg_ref[:, cols].astype(BF16)
        wu = wu_ref[:, cols].astype(BF16)
        for m in range(N_SLOT // _FFN_ROWS):
            rows = slice(m * _FFN_ROWS, (m + 1) * _FFN_ROWS)
            x = xs_ref[rows, :]
            o_ref[rows, cols] = (_silu(_dot(x, wg)) * _dot(x, wu)).astype(o_ref.dtype)


def _ffn_up(xs, w_gate, w_up, layer):
    tf = 1024
    wspec = lambda: pl.BlockSpec((None, None, D, tf), lambda e, j: (layer, e, 0, j))
    return pl.pallas_call(
        _ffn_up_kernel,
        grid=(E, D_EXP // tf),
        in_specs=[pl.BlockSpec((None, N_SLOT, D), lambda e, j: (e, 0, 0)), wspec(), wspec()],
        out_specs=pl.BlockSpec((None, N_SLOT, tf), lambda e, j: (e, 0, j)),
        out_shape=jax.ShapeDtypeStruct((E, N_SLOT, D_EXP), BF16),
        compiler_params=_cparams(("arbitrary", "arbitrary"), 58),
        name="ffn_up",
    )(xs, w_gate, w_up)


def _ffn_down_kernel(h_ref, wd_ref, gs_ref, o_ref):
    for c in range(wd_ref.shape[1] // _FFN_COLS):
        cols = slice(c * _FFN_COLS, (c + 1) * _FFN_COLS)
        wd = wd_ref[:, cols].astype(BF16)
        for m in range(N_SLOT // _FFN_ROWS):
            rows = slice(m * _FFN_ROWS, (m + 1) * _FFN_ROWS)
            gate = jnp.concatenate([gs_ref[rows, :]] * (_FFN_COLS // LANES), axis=1)
            o_ref[rows, cols] = (_dot(h_ref[rows, :], wd) * gate).astype(o_ref.dtype)


def _ffn_down(hcur, w_down, gslot, layer):
    return pl.pallas_call(
        _ffn_down_kernel,
        grid=(E,),
        in_specs=[pl.BlockSpec((None, N_SLOT, D_EXP), lambda e: (e, 0, 0)),
                  pl.BlockSpec((None, None, D_EXP, D), lambda e: (layer, e, 0, 0)),
                  pl.BlockSpec((None, N_SLOT, LANES), lambda e: (e, 0, 0))],
        out_specs=pl.BlockSpec((None, N_SLOT, D), lambda e: (e, 0, 0)),
        out_shape=jax.ShapeDtypeStruct((E, N_SLOT, D), BF16),
        compiler_params=_cparams(("arbitrary",), 48),
        name="ffn_down",
    )(hcur, w_down, gslot)


_SC_TOK = 512


def _scatter_kernel(pos_ref, ys_ref, y_ref, gate_ref, *rest, final):
    if final:
        gain_ref, oc_ref, ol_ref, p_ref = rest
    else:
        o_ref, p_ref = rest

    def emit(rows, updates, is_ctx):
        y = y_ref[rows, :] + _mod_vec(gate_ref, UNIT) * jnp.concatenate(updates, axis=1)
        if not final:
            o_ref[rows, :] = y
        else:
            r = y * lax.rsqrt(jnp.mean(y * y, axis=-1, keepdims=True) + RMS_EPS) * gain_ref[...]
            (oc_ref if is_ctx else ol_ref)[rows, :] = r

    col_chunks = [slice(c * _COL_CHUNK, (c + 1) * _COL_CHUNK) for c in range(D // _COL_CHUNK)]

    @pl.when(pl.program_id(0) < CTX_UNITS)
    def _():
        for r in range(_SC_TOK // SEQ):
            rows = slice(r * SEQ, (r + 1) * SEQ)
            first = pl.multiple_of((pl.program_id(1) * (_SC_TOK // SEQ) + r) * CAP_CTX, CAP_CTX)
            p_rows = p_ref.at[r * E * CAP_CTX:(r + 1) * E * CAP_CTX]
            _onehot_rows(pos_ref[0, :, rows], p_rows, CAP_CTX, first)
            emit(rows, [_dot_tn(p_rows[:, 0:SEQ],
                                ys_ref[:, pl.ds(first, CAP_CTX), cols].reshape(E * CAP_CTX, _COL_CHUNK))
                        for cols in col_chunks], True)

    @pl.when(pl.program_id(0) >= CTX_UNITS)
    def _():
        _onehot_rows(pos_ref[0], p_ref, SLOTS, 0)
        emit(slice(0, _SC_TOK), [_dot_tn(p_ref[...], ys_ref[:, :, cols].reshape(E * SLOTS, _COL_CHUNK))
                                 for cols in col_chunks], False)


def _scatter(pos, ys, y, mods, layer, final_gain=None):
    assert _SC_TOK % SEQ == 0 and UNIT % _SC_TOK == 0
    per_unit = UNIT // _SC_TOK
    n_ctx_tiles = N_CTX // _SC_TOK
    final = final_gain is not None
    tok = lambda: pl.BlockSpec((_SC_TOK, D), lambda u, s: (u * per_unit + s, 0))
    in_specs = [pl.BlockSpec((1, E, _SC_TOK), lambda u, s: (u, 0, s)),
                pl.BlockSpec((E, SLOTS, D), lambda u, s: (0, u, 0)),
                tok(),
                _mod_spec(layer, 5)]
    args = [pos, ys, y, mods]
    if final:
        in_specs.append(pl.BlockSpec((1, D), lambda u, s: (0, 0)))
        args.append(final_gain.reshape(1, D))
        out_specs = [pl.BlockSpec((_SC_TOK, D), lambda u, s: (jnp.minimum(u * per_unit + s, n_ctx_tiles - 1), 0)),
                     pl.BlockSpec((_SC_TOK, D), lambda u, s: (jnp.maximum(u * per_unit + s - n_ctx_tiles, 0), 0))]
        out_shape = [jax.ShapeDtypeStruct((N_CTX, D), F32), jax.ShapeDtypeStruct((N_LAT, D), F32)]
    else:
        out_specs = tok()
        out_shape = jax.ShapeDtypeStruct((N_TOK, D), F32)
    return pl.pallas_call(
        functools.partial(_scatter_kernel, final=final),
        grid=(N_UNITS, per_unit),
        in_specs=in_specs,
        out_specs=out_specs,
        out_shape=out_shape,
        scratch_shapes=[pltpu.VMEM((E * SLOTS, _SC_TOK), BF16)],
        compiler_params=_cparams(("arbitrary", "arbitrary"), 56),
        name="scatter",
    )(*args)


def kernel(x_prompt, x_sample, cache_k, cache_v, c, c_ctx, w_mod, b_mod, norm_mix, norm_ffn, w_in, rpb,
           w_pool, pool_scale, w_out, w_router, w_gate, w_up, w_down, norm_final):
    srcs = [x_prompt.reshape(N_CTX, D), x_sample.reshape(N_LAT, D)]
    cond = jnp.zeros((N_ROW, D), F32).at[0].set(c_ctx).at[1:1 + DEC_BATCH].set(c)
    mods = _adaln(cond, w_mod, b_mod)
    ck = cache_k.reshape(DEC_BATCH, DEPTH, PAST, NA_W)
    cv = cache_v.reshape(DEC_BATCH, DEPTH, PAST, NA_W)

    new_kv = []
    for l in range(DEPTH):
        q, k, v, u, *new_kv = _inproj(srcs, mods, norm_mix, _to_bf16(w_in, l), l, new_kv)
        att = _attention(q, k, v, ck, cv, rpb, l)
        pool = _pool_mixer(u, w_pool, pool_scale, l)

        w_router_pad = jnp.pad(w_router[l], ((0, 0), (0, LANES - E))).astype(BF16)
        y, h, logits = _outproj(srcs, att, pool, mods, norm_ffn, _to_bf16(w_out, l), w_router_pad, l)

        pos, aff = _route(logits)
        xs, gslot = _gather(pos, aff, h)
        ys = _ffn_down(_ffn_up(xs, w_gate, w_up, l), w_down, gslot, l)
        srcs = _scatter(pos, ys, y, mods, l, norm_final if l == DEPTH - 1 else None)
        srcs = list(srcs) if l == DEPTH - 1 else [srcs]

    y_prompt, y_sample = srcs
    new_k, new_v = new_kv
    return (y_prompt.reshape(BATCH, SEQ, D), y_sample.reshape(DEC_BATCH, DEC_SEQ, D),
            new_k.reshape(BATCH, DEPTH, SEQ, H, DH), new_v.reshape(BATCH, DEPTH, SEQ, H, DH))
```

```python
import functools

import jax
import jax.numpy as jnp
from jax import lax
from jax.experimental import pallas as pl
from jax.experimental.pallas import tpu as pltpu

F32 = jnp.float32
BF16 = jnp.bfloat16

D = 2048
BATCH, SEQ = 16, 256
DEPTH = 2
DEC_BATCH, DEC_SEQ = 8, 1024
PAST = 512
GRID_W = 64
GRID_H = DEC_SEQ // GRID_W
H, DH = 16, 64
NA_W = H * DH
POOL_WINDOWS = (2, 4, 8, 16)
PG = 256
POOL_W = 1024
IN_W = 3 * NA_W + POOL_W
NA_ROWS, NA_COLS = 8, 16
E = 16
D_EXP = 1024
N_MOD = 6
RMS_EPS = 1e-6
NEG_INF = -1e30

N_CTX = BATCH * SEQ
N_LAT = DEC_BATCH * DEC_SEQ
N_TOK = N_CTX + N_LAT
UNIT = 1024
N_UNITS = N_TOK // UNIT
CTX_UNITS = N_CTX // UNIT
REQ_PER_CTX_UNIT = UNIT // SEQ
SLOTS = 128
CAP_CTX = 2 * SEQ // E
CAP_LAT = 2 * DEC_SEQ // E
N_SLOT = N_UNITS * SLOTS
N_ROW = 16
LANES = 128
MIB = 1024 * 1024


def _cparams(sem, vmem_mib):
    return pltpu.CompilerParams(dimension_semantics=sem, vmem_limit_bytes=vmem_mib * MIB)


def _resident(block_shape, index_map):
    return pl.BlockSpec(block_shape, index_map, pipeline_mode=pl.Buffered(1))


def _dot(a, b):
    return jnp.dot(a, b, preferred_element_type=F32)


def _dot_nt(a, b):
    return lax.dot_general(a, b, (((1,), (1,)), ((), ())), preferred_element_type=F32)


def _dot_tn(a, b):
    return lax.dot_general(a, b, (((0,), (0,)), ((), ())), preferred_element_type=F32)


def _silu(x):
    return x / (1.0 + jnp.exp(-x))


def _norm_mod(x, gain, shift, scale):
    y = x * lax.rsqrt(jnp.mean(x * x, axis=-1, keepdims=True) + RMS_EPS)
    return (y * gain) * (1.0 + scale) + shift


def _mod_row_of_tile(i, tile):
    per_req = DEC_SEQ // tile
    n_ctx_tiles = N_CTX // tile
    return jnp.where(i < n_ctx_tiles, 0, 1 + (i - n_ctx_tiles) // per_req)


def _mod_spec(layer, which):
    return pl.BlockSpec((None, N_ROW, D), lambda *ids: (layer, 0, which))


def _mod_vec(ref, tile, axis=0):
    return ref[pl.ds(_mod_row_of_tile(pl.program_id(axis), tile), 1), :]


def _gain_spec():
    return pl.BlockSpec((DEPTH, D), lambda *ids: (0, 0))


def _token_specs(srcs, tile, width):
    if len(srcs) == 1:
        return [pl.BlockSpec((tile, width), lambda i: (i, 0))]
    n_ctx_tiles = N_CTX // tile
    return [pl.BlockSpec((tile, width), lambda i: (jnp.minimum(i, n_ctx_tiles - 1), 0)),
            pl.BlockSpec((tile, width), lambda i: (jnp.maximum(i - n_ctx_tiles, 0), 0))]


def _read_tokens(refs, tile, rows=slice(None)):
    if len(refs) == 1:
        return refs[0][rows, :]
    return jnp.where(pl.program_id(0) < N_CTX // tile, refs[0][rows, :], refs[1][rows, :])


def _adaln_kernel(c_ref, w_ref, b_ref, o_ref):
    a = _silu(c_ref[...]).astype(BF16)
    o_ref[0] = _dot(a, w_ref[0].astype(BF16)) + b_ref[pl.ds(pl.program_id(0), 1), :]


def _adaln(cond, w_mod, b_mod):
    tn = 1024
    return pl.pallas_call(
        _adaln_kernel,
        grid=(DEPTH, N_MOD * D // tn),
        in_specs=[pl.BlockSpec((N_ROW, D), lambda l, j: (0, 0)),
                  pl.BlockSpec((1, D, tn), lambda l, j: (l, 0, j)),
                  pl.BlockSpec((DEPTH, tn), lambda l, j: (0, j))],
        out_specs=pl.BlockSpec((1, N_ROW, tn), lambda l, j: (l, 0, j)),
        out_shape=jax.ShapeDtypeStruct((DEPTH, N_ROW, N_MOD * D), F32),
        compiler_params=_cparams(("arbitrary", "arbitrary"), 40),
        name="adaln",
    )(cond, w_mod, b_mod)


def _cast_kernel(w_ref, o_ref):
    o_ref[...] = w_ref[...].astype(BF16)


def _to_bf16(w, layer):
    rows, cols = w.shape[1:]
    tr = 512
    return pl.pallas_call(
        _cast_kernel,
        grid=(rows // tr,),
        in_specs=[pl.BlockSpec((None, tr, cols), lambda i: (layer, i, 0))],
        out_specs=pl.BlockSpec((tr, cols), lambda i: (i, 0)),
        out_shape=jax.ShapeDtypeStruct((rows, cols), BF16),
        compiler_params=_cparams(("arbitrary",), 32),
        name="cast_bf16",
    )(w)


_INPROJ_TILE = 256


def _inproj_kernel(*refs, n_src, n_prev, layer, tm):
    x_refs = refs[:n_src]
    g_ref, sh_ref, sc_ref, w_ref = refs[n_src:n_src + 4]
    q_ref, k_ref, v_ref, u_ref, kf_ref, vf_ref = refs[n_src + 4 + n_prev:]
    for r in range(tm // SEQ):
        rows = slice(r * SEQ, (r + 1) * SEQ)
        h = _norm_mod(_read_tokens(x_refs, tm, rows), g_ref[layer:layer + 1, :], _mod_vec(sh_ref, tm),
                      _mod_vec(sc_ref, tm)).astype(BF16)
        q_ref[rows, :] = (_dot(h, w_ref[:, 0:NA_W]) * (DH ** -0.5)).astype(BF16)
        k = _dot(h, w_ref[:, NA_W:2 * NA_W])
        v = _dot(h, w_ref[:, 2 * NA_W:3 * NA_W])
        k_ref[rows, :] = k.astype(BF16)
        v_ref[rows, :] = v.astype(BF16)
        u_ref[rows, :] = _dot(h, w_ref[:, 3 * NA_W:IN_W])

        @pl.when(pl.program_id(0) < N_CTX // tm)
        def _():
            for slab in range(kf_ref.shape[1]):
                mine = slab == (layer if n_prev == 0 else 0)
                kf_ref[r, slab] = k if mine else jnp.zeros_like(k)
                vf_ref[r, slab] = v if mine else jnp.zeros_like(v)


def _inproj(srcs, mods, norm_gain, w_in_bf, layer, kv_prev):
    tm = _INPROJ_TILE if len(srcs) > 1 else 2 * _INPROJ_TILE
    n_ctx_tiles = N_CTX // tm
    tok = lambda: pl.BlockSpec((tm, NA_W), lambda i: (i, 0))
    if kv_prev:
        ctx_only = lambda: pl.BlockSpec((tm // SEQ, 1, SEQ, NA_W),
                                        lambda i: (jnp.minimum(i, n_ctx_tiles - 1), layer, 0, 0))
    else:
        ctx_only = lambda: pl.BlockSpec((tm // SEQ, DEPTH, SEQ, NA_W),
                                        lambda i: (jnp.minimum(i, n_ctx_tiles - 1), 0, 0, 0))
    n_in = len(srcs) + 4
    return pl.pallas_call(
        functools.partial(_inproj_kernel, n_src=len(srcs), n_prev=len(kv_prev), layer=layer, tm=tm),
        grid=(N_TOK // tm,),
        in_specs=_token_specs(srcs, tm, D) + [
            _gain_spec(), _mod_spec(layer, 0), _mod_spec(layer, 1),
            _resident((D, IN_W), lambda i: (0, 0))] + [pl.BlockSpec(memory_space=pl.ANY)] * len(kv_prev),
        out_specs=[tok(), tok(), tok(), tok(), ctx_only(), ctx_only()],
        out_shape=[jax.ShapeDtypeStruct((N_TOK, NA_W), BF16)] * 3
        + [jax.ShapeDtypeStruct((N_TOK, POOL_W), F32)]
        + [jax.ShapeDtypeStruct((BATCH, DEPTH, SEQ, NA_W), F32)] * 2,
        input_output_aliases={n_in + j: 4 + j for j in range(len(kv_prev))},
        compiler_params=_cparams(("arbitrary",), 56),
        name="inproj",
    )(*srcs, norm_gain, mods, mods, w_in_bf, *kv_prev)


def _na_window_start(qr):
    return min(max(qr - NA_ROWS // 2, 0), GRID_H - NA_ROWS)


def _fill_na_bias(rpb_ref, o_ref):
    assert LANES == 2 * GRID_W
    qc = lax.broadcasted_iota(jnp.int32, (GRID_W, LANES), 0)
    lane = lax.broadcasted_iota(jnp.int32, (GRID_W, LANES), 1)
    kc = jnp.bitwise_and(lane, GRID_W - 1)
    cs = jnp.clip(qc - NA_COLS // 2, 0, GRID_W - NA_COLS)
    in_col = (kc >= cs) & (kc < cs + NA_COLS)
    keep_lanes = {(True, True): in_col, (True, False): in_col & (lane < GRID_W),
                  (False, True): in_col & (lane >= GRID_W)}
    n_dr = 2 * NA_ROWS - 1
    zero_rows = jnp.zeros((GRID_W, LANES), F32)

    def rpb_rows(dr):
        if not 0 <= dr < n_dr:
            return zero_rows
        return jnp.broadcast_to(rpb_ref[dr:dr + 1, :], (GRID_W, LANES))

    pair_cache = {}

    def pair_tile(dr, use_first, use_second):
        if not (use_first or use_second):
            return jnp.full((GRID_W, LANES), NEG_INF, F32)
        key = (dr, use_first, use_second)
        if key not in pair_cache:
            src = rpb_rows(dr) + pltpu.roll(rpb_rows(dr + 1), GRID_W, 1)
            toep = pltpu.roll(src, LANES - (NA_COLS - 1), 1, stride=1, stride_axis=0)
            pair_cache[key] = jnp.where(keep_lanes[(use_first, use_second)], toep, NEG_INF)
        return pair_cache[key]

    for qr in range(GRID_H):
        rs = _na_window_start(qr)
        for kr in range(0, GRID_H, 2):
            use = [rs <= k < rs + NA_ROWS for k in (kr, kr + 1)]
            o_ref[qr * GRID_W:(qr + 1) * GRID_W, kr * GRID_W:(kr + 2) * GRID_W] = pair_tile(
                kr - qr + NA_ROWS - 1, use[0], use[1])


_NA_Q_BLOCK = 4 * GRID_W


def _na_key_range(qb):
    rows_per_block = _NA_Q_BLOCK // GRID_W
    lo = _na_window_start(qb * rows_per_block) * GRID_W
    hi = (_na_window_start((qb + 1) * rows_per_block - 1) + NA_ROWS) * GRID_W
    return (lo // _NA_Q_BLOCK * _NA_Q_BLOCK, -(-hi // _NA_Q_BLOCK) * _NA_Q_BLOCK)


_NA_KEY_RANGES = [_na_key_range(qb) for qb in range(DEC_SEQ // _NA_Q_BLOCK)]


_PAIRS_PER_STEP = 4


def _attn_kernel(q_ref, k_ref, v_ref, ck_ref, cv_ref, rpb_ref, o_ref, kc_ref, vv_ref, tab_ref):
    @pl.when(pl.program_id(1) == CTX_UNITS)
    def _():
        for h in range(2 * _PAIRS_PER_STEP):
            _fill_na_bias(rpb_ref.at[h], tab_ref.at[h])

    lane = lax.broadcasted_iota(jnp.int32, (1, 2 * DH), 1)
    head_lanes = [lane < DH, lane >= DH]

    def probs(scores):
        m = functools.reduce(jnp.maximum, [jnp.max(s, axis=-1, keepdims=True) for s in scores])
        return [jnp.exp(s - m).astype(BF16) for s in scores]

    def merge_heads(acc):
        outs = [a[:, 0:2 * DH] / a[:, 2 * DH:4 * DH] for a in acc]
        return jnp.where(head_lanes[0], outs[0], outs[1]).astype(o_ref.dtype)

    pairs = [(pp, slice(pp * 2 * DH, (pp + 1) * 2 * DH)) for pp in range(_PAIRS_PER_STEP)]
    for pp, pair in pairs:
        vv_ref[pp, 0:UNIT, 0:2 * DH] = v_ref[:, pair]
        vv_ref[pp, :, 2 * DH:4 * DH] = jnp.ones((UNIT + PAST, 2 * DH), BF16)

    def head_queries(rows, pair):
        q = q_ref[rows, pair]
        return [jnp.where(m, q, jnp.zeros_like(q)) for m in head_lanes]

    @pl.when(pl.program_id(1) < CTX_UNITS)
    def _():
        for pp, pair in pairs:
            for r in range(REQ_PER_CTX_UNIT):
                rows = slice(r * SEQ, (r + 1) * SEQ)
                acc = []
                for qh in head_queries(rows, pair):
                    (p,) = probs([_dot_nt(qh, k_ref[rows, pair])])
                    acc.append(_dot(p, vv_ref[pp, rows, :]))
                o_ref[rows, pair] = merge_heads(acc)

    @pl.when(pl.program_id(1) >= CTX_UNITS)
    def _():
        for pp, pair in pairs:
            kc_ref[pp] = ck_ref[0, 0, :, pair].astype(BF16)
            vv_ref[pp, UNIT:UNIT + PAST, 0:2 * DH] = cv_ref[0, 0, :, pair].astype(BF16)
            for qb, (k_lo, k_hi) in enumerate(_NA_KEY_RANGES):
                rows = slice(qb * _NA_Q_BLOCK, (qb + 1) * _NA_Q_BLOCK)
                keys = slice(k_lo, k_hi)
                acc = []
                for hh, qh in enumerate(head_queries(rows, pair)):
                    s_lat = _dot_nt(qh, k_ref[keys, pair]) + tab_ref[2 * pp + hh, rows, keys]
                    s_ctx = _dot_nt(qh, kc_ref[pp])
                    p_lat, p_ctx = probs([s_lat, s_ctx])
                    acc.append(_dot(p_lat, vv_ref[pp, keys, :]) + _dot(p_ctx, vv_ref[pp, UNIT:UNIT + PAST, :]))
                o_ref[rows, pair] = merge_heads(acc)


def _attention(q, k, v, cache_k, cache_v, rpb, layer):
    width = _PAIRS_PER_STEP * 2 * DH
    blk = lambda: pl.BlockSpec((UNIT, width), lambda g, u: (u, g))
    cache = lambda: pl.BlockSpec((1, 1, PAST, width), lambda g, u: (jnp.maximum(u - CTX_UNITS, 0), layer, 0, g))
    rpb_pad = jnp.pad(rpb, ((0, 0), (0, 0), (0, 0), (0, LANES - rpb.shape[-1])))
    return pl.pallas_call(
        _attn_kernel,
        grid=(NA_W // width, N_UNITS),
        in_specs=[blk(), blk(), blk(), cache(), cache(),
                  pl.BlockSpec((None, 2 * _PAIRS_PER_STEP, rpb.shape[2], LANES), lambda g, u: (layer, g, 0, 0))],
        out_specs=blk(),
        out_shape=jax.ShapeDtypeStruct((N_TOK, NA_W), BF16),
        scratch_shapes=[pltpu.VMEM((_PAIRS_PER_STEP, PAST, 2 * DH), BF16),
                        pltpu.VMEM((_PAIRS_PER_STEP, UNIT + PAST, 4 * DH), BF16),
                        pltpu.VMEM((2 * _PAIRS_PER_STEP, DEC_SEQ, DEC_SEQ), F32)],
        compiler_params=_cparams(("arbitrary", "arbitrary"), 56),
        name="attention",
    )(q, k, v, cache_k, cache_v, rpb_pad)


_POOL_EDGE = 8


def _pool_kernel(u_ref, wp_ref, ps_ref, o_ref, *, layer):
    n = jnp.where(pl.program_id(0) < CTX_UNITS, SEQ, DEC_SEQ)
    assert max(POOL_WINDOWS) // 2 <= _POOL_EDGE
    blocks = range(UNIT // SEQ)

    def edge_pos(first_row):
        t = lax.broadcasted_iota(jnp.int32, (_POOL_EDGE, PG), 0) + first_row
        return jnp.bitwise_and(t, n - 1)

    head_pos = [edge_pos(k * SEQ) for k in blocks]
    tail_pos = [edge_pos((k + 1) * SEQ - _POOL_EDGE) for k in blocks]

    def patch(x, head_fn=None, tail_fn=None, body_fn=None):
        pieces = []
        for k in blocks:
            lo, hi = k * SEQ, (k + 1) * SEQ
            head, body, tail = x[lo:lo + _POOL_EDGE], x[lo + _POOL_EDGE:hi - _POOL_EDGE], x[hi - _POOL_EDGE:hi]
            pieces += [head_fn(head, k) if head_fn else head, body_fn(body) if body_fn else body,
                       tail_fn(tail, k) if tail_fn else tail]
        return jnp.concatenate(pieces, axis=0)

    def fwd(a, s):
        return patch(pltpu.roll(a, UNIT - s, 0), tail_fn=lambda x, k: jnp.where(tail_pos[k] < n - s, x, 0.0))

    def bwd(a, s):
        return patch(pltpu.roll(a, s, 0), head_fn=lambda x, k: jnp.where(head_pos[k] >= s, x, 0.0))

    for gi, w in enumerate(POOL_WINDOWS):
        half = w // 2
        cols = slice(gi * PG, (gi + 1) * PG)
        g = u_ref[:, cols]
        f, b, s = g, g, 1
        while s < half:
            f = f + fwd(f, s)
            b = b + bwd(b, s)
            s *= 2
        tot = f + bwd(b, 1)

        def edge_mean(x, pos):
            cnt = jnp.minimum(pos + half, n) - jnp.maximum(pos - half, 0)
            return x / cnt.astype(F32)

        mean = patch(tot, head_fn=lambda x, k: edge_mean(x, head_pos[k]),
                     tail_fn=lambda x, k: edge_mean(x, tail_pos[k]), body_fn=lambda x: x * (1.0 / w))
        dlt = mean - g
        out = _dot(dlt.astype(BF16), wp_ref[gi].astype(BF16)) * ps_ref[layer:layer + 1, cols]
        o_ref[:, cols] = out.astype(o_ref.dtype)


def _pool_mixer(u, w_pool, pool_scale, layer):
    return pl.pallas_call(
        functools.partial(_pool_kernel, layer=layer),
        grid=(N_UNITS,),
        in_specs=[pl.BlockSpec((UNIT, POOL_W), lambda i: (i, 0)),
                  pl.BlockSpec((None, len(POOL_WINDOWS), PG, PG), lambda i: (layer, 0, 0, 0)),
                  pl.BlockSpec((DEPTH, POOL_W), lambda i: (0, 0))],
        out_specs=pl.BlockSpec((UNIT, POOL_W), lambda i: (i, 0)),
        out_shape=jax.ShapeDtypeStruct((N_TOK, POOL_W), BF16),
        compiler_params=_cparams(("arbitrary",), 40),
        name="pool",
    )(u, w_pool, pool_scale)


_OUTPROJ_TILE = 512
_OUTPROJ_ROWS = 256


_W_STAGE_ROWS = 256


def _stage_weight_bf16(w_hbm, layer, stage_ref, sem_ref, w_ref):
    n_chunks = w_ref.shape[0] // _W_STAGE_ROWS

    def copy(c):
        return pltpu.make_async_copy(w_hbm.at[layer, pl.ds(c * _W_STAGE_ROWS, _W_STAGE_ROWS), :],
                                     stage_ref.at[c % 2], sem_ref.at[c % 2])

    copy(0).start()
    for c in range(n_chunks):
        if c + 1 < n_chunks:
            copy(c + 1).start()
        copy(c).wait()
        w_ref[c * _W_STAGE_ROWS:(c + 1) * _W_STAGE_ROWS, :] = stage_ref[c % 2].astype(BF16)


def _outproj_kernel(*refs, n_src, layer):
    x_refs = refs[:n_src]
    (att_ref, pool_ref, w_hbm, gate_ref, g2_ref, sh_ref, sc_ref, wr_ref, y_ref, h_ref, lg_ref,
     w_ref, stage_ref, sem_ref) = refs[n_src:]

    @pl.when(pl.program_id(0) == 0)
    def _():
        _stage_weight_bf16(w_hbm, layer, stage_ref, sem_ref, w_ref)

    tm = _OUTPROJ_TILE
    for r in range(tm // _OUTPROJ_ROWS):
        rows = slice(r * _OUTPROJ_ROWS, (r + 1) * _OUTPROJ_ROWS)
        mix = _dot(att_ref[rows, :], w_ref[0:NA_W, :]) + _dot(pool_ref[rows, :], w_ref[NA_W:NA_W + POOL_W, :])
        y = _read_tokens(x_refs, tm, rows) + _mod_vec(gate_ref, tm) * mix
        y_ref[rows, :] = y
        h = _norm_mod(y, g2_ref[layer:layer + 1, :], _mod_vec(sh_ref, tm), _mod_vec(sc_ref, tm)).astype(BF16)
        h_ref[rows, :] = h
        lg_ref[rows, :] = _dot(h, wr_ref[...])


def _outproj(srcs, att, pool, mods, norm_gain, w_out, w_router_pad, layer):
    tm = _OUTPROJ_TILE
    row = lambda width: pl.BlockSpec((tm, width), lambda i: (i, 0))
    return pl.pallas_call(
        functools.partial(_outproj_kernel, n_src=len(srcs), layer=layer),
        grid=(N_TOK // tm,),
        in_specs=_token_specs(srcs, tm, D) + [
            row(NA_W), row(POOL_W),
            pl.BlockSpec(memory_space=pl.ANY),
            _mod_spec(layer, 2), _gain_spec(), _mod_spec(layer, 3), _mod_spec(layer, 4),
            _resident((D, LANES), lambda i: (0, 0))],
        out_specs=[row(D), row(D), row(LANES)],
        out_shape=[jax.ShapeDtypeStruct((N_TOK, D), F32),
                   jax.ShapeDtypeStruct((N_TOK, D), BF16),
                   jax.ShapeDtypeStruct((N_TOK, LANES), F32)],
        scratch_shapes=[pltpu.VMEM((D, D), BF16),
                        pltpu.VMEM((2, _W_STAGE_ROWS, D), F32),
                        pltpu.SemaphoreType.DMA((2,))],
        compiler_params=_cparams(("arbitrary",), 52),
        name="outproj",
    )(*srcs, att, pool, w_out, mods, norm_gain, mods, mods, w_router_pad)


_SEARCH_BITS = 3


def _choose_slots(aff, n_req, n_tok, cap):
    rows = jnp.concatenate([aff[:, r * n_tok:(r + 1) * n_tok] for r in range(n_req)], axis=0)
    n_rows = n_req * E
    thr = jnp.zeros((n_rows, 1), jnp.int32)
    hi = 31
    while hi > 0:
        lo = max(hi - _SEARCH_BITS, 0)
        n_cand = (1 << (hi - lo)) - 1
        hits = jnp.concatenate([(rows >= lax.bitcast_convert_type(thr | (j << lo), F32)).astype(F32)
                                for j in range(1, n_cand + 1)], axis=0)
        reached = (jnp.sum(hits, axis=-1, keepdims=True) >= cap).astype(jnp.int32)
        group = functools.reduce(jnp.add, [reached[j * n_rows:(j + 1) * n_rows] for j in range(n_cand)])
        thr = thr | lax.shift_left(group, jnp.int32(lo))
        hi = lo
    thr_val = lax.bitcast_convert_type(thr, F32)
    gt = rows > thr_val
    eq = rows == thr_val
    n_gt = jnp.sum(gt.astype(F32), axis=-1, keepdims=True)
    before = (lax.broadcasted_iota(jnp.int32, (n_tok, n_tok), 0)
              < lax.broadcasted_iota(jnp.int32, (n_tok, n_tok), 1)).astype(BF16)
    eq_rank = _dot(eq.astype(BF16), before)
    sel = gt | (eq & (eq_rank < cap - n_gt))
    slot = _dot(sel.astype(BF16), before)
    pieces = [jnp.where(sel[r * E:(r + 1) * E], slot[r * E:(r + 1) * E] + r * cap, -1.0) for r in range(n_req)]
    return jnp.concatenate(pieces, axis=1) if n_req > 1 else pieces[0]


def _route_kernel(lg_ref, pos_ref, aff_ref):
    lg = lg_ref[...].T[0:E, :]
    ex = jnp.exp(lg - jnp.max(lg, axis=0, keepdims=True))
    aff = ex / jnp.sum(ex, axis=0, keepdims=True)
    aff_ref[0] = aff

    @pl.when(pl.program_id(0) < CTX_UNITS)
    def _():
        pos_ref[0] = _choose_slots(aff, REQ_PER_CTX_UNIT, SEQ, CAP_CTX)

    @pl.when(pl.program_id(0) >= CTX_UNITS)
    def _():
        pos_ref[0] = _choose_slots(aff, 1, DEC_SEQ, CAP_LAT)


def _route(logits):
    out_blk = lambda: pl.BlockSpec((1, E, UNIT), lambda u: (u, 0, 0))
    return pl.pallas_call(
        _route_kernel,
        grid=(N_UNITS,),
        in_specs=[pl.BlockSpec((UNIT, LANES), lambda u: (u, 0))],
        out_specs=[out_blk(), out_blk()],
        out_shape=[jax.ShapeDtypeStruct((N_UNITS, E, UNIT), F32)] * 2,
        compiler_params=_cparams(("arbitrary",), 32),
        name="route",
    )(logits)


_COL_CHUNK = 512


def _onehot_rows(pos, p_ref, n_slots, first_slot):
    width = pos.shape[1]
    slot_id = (lax.broadcasted_iota(jnp.int32, (n_slots, width), 0) + first_slot).astype(F32)
    matches = []
    for e in range(E):
        match = pos[e:e + 1, :] == slot_id
        p_ref[e * n_slots:(e + 1) * n_slots, 0:width] = jnp.where(match, 1.0, 0.0).astype(BF16)
        matches.append(match)
    return matches


def _gather_kernel(pos_ref, aff_ref, h_ref, xs_ref, gs_ref, p_ref):
    def gather(tokens, slots, first_slot):
        n = slots.stop - slots.start
        aff = aff_ref[0, :, tokens]
        matches = _onehot_rows(pos_ref[0, :, tokens], p_ref, n, first_slot)
        for e in range(E):
            gate = jnp.sum(jnp.where(matches[e], aff[e:e + 1, :], 0.0), axis=-1, keepdims=True)
            gs_ref[e, slots, :] = jnp.broadcast_to(gate, (n, LANES))
        width = tokens.stop - tokens.start
        for c in range(D // _COL_CHUNK):
            cols = slice(c * _COL_CHUNK, (c + 1) * _COL_CHUNK)
            r = _dot(p_ref[0:E * n, 0:width], h_ref[tokens, cols])
            xs_ref[:, slots, cols] = r.reshape(E, n, _COL_CHUNK).astype(BF16)

    @pl.when(pl.program_id(0) < CTX_UNITS)
    def _():
        for r in range(REQ_PER_CTX_UNIT):
            gather(slice(r * SEQ, (r + 1) * SEQ), slice(r * CAP_CTX, (r + 1) * CAP_CTX), r * CAP_CTX)

    @pl.when(pl.program_id(0) >= CTX_UNITS)
    def _():
        gather(slice(0, UNIT), slice(0, SLOTS), 0)


def _gather(pos, aff, h):
    unit3 = lambda: pl.BlockSpec((1, E, UNIT), lambda u: (u, 0, 0))
    return pl.pallas_call(
        _gather_kernel,
        grid=(N_UNITS,),
        in_specs=[unit3(), unit3(), pl.BlockSpec((UNIT, D), lambda u: (u, 0))],
        out_specs=[pl.BlockSpec((E, SLOTS, D), lambda u: (0, u, 0)),
                   pl.BlockSpec((E, SLOTS, LANES), lambda u: (0, u, 0))],
        out_shape=[jax.ShapeDtypeStruct((E, N_SLOT, D), BF16),
                   jax.ShapeDtypeStruct((E, N_SLOT, LANES), F32)],
        scratch_shapes=[pltpu.VMEM((E * SLOTS, UNIT), BF16)],
        compiler_params=_cparams(("arbitrary",), 48),
        name="gather",
    )(pos, aff, h)


_FFN_ROWS = 256
_FFN_COLS = 256


def _ffn_up_kernel(xs_ref, wg_ref, wu_ref, o_ref):
    for c in range(wg_ref.shape[1] // _FFN_COLS):
        cols = slice(c * _FFN_COLS, (c + 1) * _FFN_COLS)
        wg = wg_ref[:, cols].astype(BF16)
        wu = wu_ref[:, cols].astype(BF16)
        for m in range(N_SLOT // _FFN_ROWS):
            rows = slice(m * _FFN_ROWS, (m + 1) * _FFN_ROWS)
            x = xs_ref[rows, :]
            o_ref[rows, cols] = (_silu(_dot(x, wg)) * _dot(x, wu)).astype(o_ref.dtype)


def _ffn_up(xs, w_gate, w_up, layer):
    tf = 512
    wspec = lambda: pl.BlockSpec((None, None, D, tf), lambda e, j: (layer, e, 0, j))
    return pl.pallas_call(
        _ffn_up_kernel,
        grid=(E, D_EXP // tf),
        in_specs=[pl.BlockSpec((None, N_SLOT, D), lambda e, j: (e, 0, 0)), wspec(), wspec()],
        out_specs=pl.BlockSpec((None, N_SLOT, tf), lambda e, j: (e, 0, j)),
        out_shape=jax.ShapeDtypeStruct((E, N_SLOT, D_EXP), BF16),
        compiler_params=_cparams(("arbitrary", "arbitrary"), 48),
        name="ffn_up",
    )(xs, w_gate, w_up)


def _ffn_down_kernel(h_ref, wd_ref, gs_ref, o_ref):
    for c in range(wd_ref.shape[1] // _FFN_COLS):
        cols = slice(c * _FFN_COLS, (c + 1) * _FFN_COLS)
        wd = wd_ref[:, cols].astype(BF16)
        for m in range(N_SLOT // _FFN_ROWS):
            rows = slice(m * _FFN_ROWS, (m + 1) * _FFN_ROWS)
            gate = jnp.concatenate([gs_ref[rows, :]] * (_FFN_COLS // LANES), axis=1)
            o_ref[rows, cols] = (_dot(h_ref[rows, :], wd) * gate).astype(o_ref.dtype)


def _ffn_down(hcur, w_down, gslot, layer):
    return pl.pallas_call(
        _ffn_down_kernel,
        grid=(E,),
        in_specs=[pl.BlockSpec((None, N_SLOT, D_EXP), lambda e: (e, 0, 0)),
                  pl.BlockSpec((None, None, D_EXP, D), lambda e: (layer, e, 0, 0)),
                  pl.BlockSpec((None, N_SLOT, LANES), lambda e: (e, 0, 0))],
        out_specs=pl.BlockSpec((None, N_SLOT, D), lambda e: (e, 0, 0)),
        out_shape=jax.ShapeDtypeStruct((E, N_SLOT, D), BF16),
        compiler_params=_cparams(("arbitrary",), 48),
        name="ffn_down",
    )(hcur, w_down, gslot)


_SC_TOK = 512


def _scatter_kernel(pos_ref, ys_ref, y_ref, gate_ref, *rest, final):
    if final:
        gain_ref, oc_ref, ol_ref, p_ref = rest
    else:
        o_ref, p_ref = rest

    def emit(rows, updates, is_ctx):
        y = y_ref[rows, :] + _mod_vec(gate_ref, UNIT) * jnp.concatenate(updates, axis=1)
        if not final:
            o_ref[rows, :] = y
        else:
            r = y * lax.rsqrt(jnp.mean(y * y, axis=-1, keepdims=True) + RMS_EPS) * gain_ref[...]
            (oc_ref if is_ctx else ol_ref)[rows, :] = r

    col_chunks = [slice(c * _COL_CHUNK, (c + 1) * _COL_CHUNK) for c in range(D // _COL_CHUNK)]

    @pl.when(pl.program_id(0) < CTX_UNITS)
    def _():
        for r in range(_SC_TOK // SEQ):
            rows = slice(r * SEQ, (r + 1) * SEQ)
            first = pl.multiple_of((pl.program_id(1) * (_SC_TOK // SEQ) + r) * CAP_CTX, CAP_CTX)
            p_rows = p_ref.at[r * E * CAP_CTX:(r + 1) * E * CAP_CTX]
            _onehot_rows(pos_ref[0, :, rows], p_rows, CAP_CTX, first)
            emit(rows, [_dot_tn(p_rows[:, 0:SEQ],
                                ys_ref[:, pl.ds(first, CAP_CTX), cols].reshape(E * CAP_CTX, _COL_CHUNK))
                        for cols in col_chunks], True)

    @pl.when(pl.program_id(0) >= CTX_UNITS)
    def _():
        _onehot_rows(pos_ref[0], p_ref, SLOTS, 0)
        emit(slice(0, _SC_TOK), [_dot_tn(p_ref[...], ys_ref[:, :, cols].reshape(E * SLOTS, _COL_CHUNK))
                                 for cols in col_chunks], False)


def _scatter(pos, ys, y, mods, layer, final_gain=None):
    assert _SC_TOK % SEQ == 0 and UNIT % _SC_TOK == 0
    per_unit = UNIT // _SC_TOK
    n_ctx_tiles = N_CTX // _SC_TOK
    final = final_gain is not None
    tok = lambda: pl.BlockSpec((_SC_TOK, D), lambda u, s: (u * per_unit + s, 0))
    in_specs = [pl.BlockSpec((1, E, _SC_TOK), lambda u, s: (u, 0, s)),
                pl.BlockSpec((E, SLOTS, D), lambda u, s: (0, u, 0)),
                tok(),
                _mod_spec(layer, 5)]
    args = [pos, ys, y, mods]
    if final:
        in_specs.append(pl.BlockSpec((1, D), lambda u, s: (0, 0)))
        args.append(final_gain.reshape(1, D))
        out_specs = [pl.BlockSpec((_SC_TOK, D), lambda u, s: (jnp.minimum(u * per_unit + s, n_ctx_tiles - 1), 0)),
                     pl.BlockSpec((_SC_TOK, D), lambda u, s: (jnp.maximum(u * per_unit + s - n_ctx_tiles, 0), 0))]
        out_shape = [jax.ShapeDtypeStruct((N_CTX, D), F32), jax.ShapeDtypeStruct((N_LAT, D), F32)]
    else:
        out_specs = tok()
        out_shape = jax.ShapeDtypeStruct((N_TOK, D), F32)
    return pl.pallas_call(
        functools.partial(_scatter_kernel, final=final),
        grid=(N_UNITS, per_unit),
        in_specs=in_specs,
        out_specs=out_specs,
        out_shape=out_shape,
        scratch_shapes=[pltpu.VMEM((E * SLOTS, _SC_TOK), BF16)],
        compiler_params=_cparams(("arbitrary", "arbitrary"), 56),
        name="scatter",
    )(*args)


def kernel(x_prompt, x_sample, cache_k, cache_v, c, c_ctx, w_mod, b_mod, norm_mix, norm_ffn, w_in, rpb,
           w_pool, pool_scale, w_out, w_router, w_gate, w_up, w_down, norm_final):
    srcs = [x_prompt.reshape(N_CTX, D), x_sample.reshape(N_LAT, D)]
    cond = jnp.zeros((N_ROW, D), F32).at[0].set(c_ctx).at[1:1 + DEC_BATCH].set(c)
    mods = _adaln(cond, w_mod, b_mod)
    ck = cache_k.reshape(DEC_BATCH, DEPTH, PAST, NA_W)
    cv = cache_v.reshape(DEC_BATCH, DEPTH, PAST, NA_W)

    new_kv = []
    for l in range(DEPTH):
        q, k, v, u, *new_kv = _inproj(srcs, mods, norm_mix, _to_bf16(w_in, l), l, new_kv)
        att = _attention(q, k, v, ck, cv, rpb, l)
        pool = _pool_mixer(u, w_pool, pool_scale, l)

        w_router_pad = jnp.pad(w_router[l], ((0, 0), (0, LANES - E))).astype(BF16)
        y, h, logits = _outproj(srcs, att, pool, mods, norm_ffn, w_out, w_router_pad, l)

        pos, aff = _route(logits)
        xs, gslot = _gather(pos, aff, h)
        ys = _ffn_down(_ffn_up(xs, w_gate, w_up, l), w_down, gslot, l)
        srcs = _scatter(pos, ys, y, mods, l, norm_final if l == DEPTH - 1 else None)
        srcs = list(srcs) if l == DEPTH - 1 else [srcs]

    y_prompt, y_sample = srcs
    new_k, new_v = new_kv
    return (y_prompt.reshape(BATCH, SEQ, D), y_sample.reshape(DEC_BATCH, DEC_SEQ, D),
            new_k.reshape(BATCH, DEPTH, SEQ, H, DH), new_v.reshape(BATCH, DEPTH, SEQ, H, DH))
```
